```python
import jax
import jax.numpy as jnp
from jax import lax
import numpy as np


D_MODEL = 1024
BATCH = 8
SEQ = 2048
DEPTH = 4

HEAD_DIM = 64
FOX_HEADS = 8
SB_HEADS = 8
FOX_WIDTH = FOX_HEADS * HEAD_DIM
SB_WIDTH = SB_HEADS * HEAD_DIM
CONV_WIDTH = 512
CONV_K = 3
N_BRANCH = 3
D_FF = 2816
Q_BLOCK = 128
NORM_EPS = 1e-6
NEG_INF = -1e30

SPLIT_POINTS = [
    3 * CONV_WIDTH,
    3 * CONV_WIDTH + 3 * FOX_WIDTH,
    3 * CONV_WIDTH + 3 * FOX_WIDTH + FOX_HEADS,
    3 * CONV_WIDTH + 3 * FOX_WIDTH + FOX_HEADS + 3 * SB_WIDTH,
]
D_IN = 3 * CONV_WIDTH + 3 * FOX_WIDTH + FOX_HEADS + 3 * SB_WIDTH + N_BRANCH * D_MODEL

kernel_name = 'hybrid_gatedconv_fox_stickbreak_convglu'


def rmsnorm(x, g):
    xf = x.astype(jnp.float32)
    y = xf * lax.rsqrt(jnp.mean(xf * xf, axis=-1, keepdims=True) + NORM_EPS)
    return (y * g.astype(jnp.float32)).astype(x.dtype)


def causal_dwconv(x, w):
    k = w.shape[0]
    return lax.conv_general_dilated(
        x, w[:, None, :].astype(x.dtype), window_strides=(1,), padding=[(k - 1, 0)],
        dimension_numbers=('NWC', 'WIO', 'NWC'), feature_group_count=x.shape[-1])


def _heads(t, n):
    b, s, _ = t.shape
    return t.reshape(b, s, n, HEAD_DIM).transpose(0, 2, 1, 3)


def _to_blocks(t):
    b, h, s = t.shape[:3]
    t = t.reshape(b, h, s // Q_BLOCK, Q_BLOCK, *t.shape[3:])
    return jnp.moveaxis(t, 2, 0)


def _from_blocks(o):
    nb, b, h, q, d = o.shape
    return o.transpose(1, 0, 3, 2, 4).reshape(b, nb * q, h * d)


def short_conv_mixer(b_gate, c_gate, h, conv_w):
    return b_gate * causal_dwconv(c_gate * h, conv_w)


def forgetting_attention(q, k, v, log_f, qn_g, kn_g):
    q = rmsnorm(_heads(q, FOX_HEADS), qn_g)
    k = rmsnorm(_heads(k, FOX_HEADS), kn_g)
    v = _heads(v, FOX_HEADS)
    c = lax.cumsum(log_f, axis=1).transpose(0, 2, 1)
    pos = jnp.arange(q.shape[2])
    scale = HEAD_DIM ** -0.5

    def block(args):
        qb, cqb, qpos = args
        logits = (jnp.einsum('bhqd,bhkd->bhqk', qb, k).astype(jnp.float32) * scale
                  + cqb[..., None] - c[:, :, None, :])
        logits = jnp.where(pos[None, :] <= qpos[:, None], logits, NEG_INF)
        p = jax.nn.softmax(logits, axis=-1)
        return jnp.einsum('bhqk,bhkd->bhqd', p.astype(v.dtype), v)

    out = lax.map(block, (_to_blocks(q), _to_blocks(c), pos.reshape(-1, Q_BLOCK)))
    return _from_blocks(out)


def stick_breaking_attention(q, k, v):
    q = _heads(q, SB_HEADS)
    k = _heads(k, SB_HEADS)
    v = _heads(v, SB_HEADS)
    pos = jnp.arange(q.shape[2])
    scale = HEAD_DIM ** -0.5

    def block(args):
        qb, qpos = args
        z = jnp.einsum('bhqd,bhkd->bhqk', qb, k).astype(jnp.float32) * scale
        strict = pos[None, :] < qpos[:, None]
        log_1m = jnp.where(strict, jax.nn.log_sigmoid(-z), 0.0)
        suffix = lax.cumsum(log_1m, axis=log_1m.ndim - 1, reverse=True) - log_1m
        a = jnp.where(strict, jnp.exp(jax.nn.log_sigmoid(z) + suffix), 0.0)
        return jnp.einsum('bhqk,bhkd->bhqd', a.astype(v.dtype), v)

    out = lax.map(block, (_to_blocks(q), pos.reshape(-1, Q_BLOCK)))
    return _from_blocks(out)


def setup_inputs(seed: int = 0) -> dict:
    key = jax.random.key(seed)
    ks = jax.random.split(key, 17)
    f32 = jnp.float32
    nrm = lambda k, shape, s: jax.random.normal(k, shape, f32) * s
    return {
        'x': nrm(ks[0], (BATCH, SEQ, D_MODEL), 1.0),
        'norm1_g': 1.0 + nrm(ks[1], (DEPTH, D_MODEL), 0.05),
        'w_in': nrm(ks[2], (DEPTH, D_MODEL, D_IN), D_MODEL ** -0.5),
        'fox_f_bias': 2.0 + nrm(ks[3], (DEPTH, FOX_HEADS), 0.5),
        'gate_bias': nrm(ks[4], (DEPTH, N_BRANCH * D_MODEL), 0.1),
        'conv_w': nrm(ks[5], (DEPTH, CONV_K, CONV_WIDTH), CONV_K ** -0.5),
        'fox_q_norm_g': 1.0 + nrm(ks[6], (DEPTH, HEAD_DIM), 0.05),
        'fox_k_norm_g': 1.0 + nrm(ks[7], (DEPTH, HEAD_DIM), 0.05),
        'w_proj_conv': nrm(ks[8], (DEPTH, CONV_WIDTH, D_MODEL), CONV_WIDTH ** -0.5),
        'w_proj_fox': nrm(ks[9], (DEPTH, FOX_WIDTH, D_MODEL), FOX_WIDTH ** -0.5),
        'w_proj_sb': nrm(ks[10], (DEPTH, SB_WIDTH, D_MODEL), SB_WIDTH ** -0.5),
        'w_out': nrm(ks[11], (DEPTH, D_MODEL, D_MODEL), D_MODEL ** -0.5),
        'norm2_g': 1.0 + nrm(ks[12], (DEPTH, D_MODEL), 0.05),
        'w_up': nrm(ks[13], (DEPTH, D_MODEL, 2 * D_FF), D_MODEL ** -0.5),
        'ffn_conv_w': nrm(ks[14], (DEPTH, CONV_K, D_FF), CONV_K ** -0.5),
        'ffn_conv_b': nrm(ks[15], (DEPTH, D_FF), 0.02),
        'w_down': nrm(ks[16], (DEPTH, D_FF, D_MODEL), D_FF ** -0.5),
    }


def reference(x, norm1_g, w_in, fox_f_bias, gate_bias, conv_w, fox_q_norm_g, fox_k_norm_g,
              w_proj_conv, w_proj_fox, w_proj_sb, w_out, norm2_g, w_up, ffn_conv_w,
              ffn_conv_b, w_down):
    for l in range(DEPTH):
        hn = rmsnorm(x, norm1_g[l])
        proj = hn @ w_in[l]
        conv_bch, fox_qkv, fox_f, sb_qkv, gate_logits = jnp.split(proj, SPLIT_POINTS, axis=-1)

        cb, cc, ch = jnp.split(conv_bch, 3, axis=-1)
        y_conv = short_conv_mixer(cb, cc, ch, conv_w[l]) @ w_proj_conv[l]

        fq, fk, fv = jnp.split(fox_qkv, 3, axis=-1)
        log_f = jax.nn.log_sigmoid(fox_f.astype(jnp.float32) + fox_f_bias[l].astype(jnp.float32))
        y_fox = forgetting_attention(fq, fk, fv, log_f, fox_q_norm_g[l], fox_k_norm_g[l]) @ w_proj_fox[l]

        sq, sk, sv = jnp.split(sb_qkv, 3, axis=-1)
        y_sb = stick_breaking_attention(sq, sk, sv) @ w_proj_sb[l]

        g_conv, g_fox, g_sb = jnp.split(jax.nn.sigmoid(gate_logits + gate_bias[l]), 3, axis=-1)
        x = x + (g_conv * y_conv + g_fox * y_fox + g_sb * y_sb) @ w_out[l]

        hn = rmsnorm(x, norm2_g[l])
        u_gate, u_val = jnp.split(hn @ w_up[l], 2, axis=-1)
        act = jax.nn.silu(causal_dwconv(u_gate, ffn_conv_w[l]) + ffn_conv_b[l])
        x = x + (act * u_val) @ w_down[l]
    return x
```

```python
import functools

import jax
import jax.numpy as jnp
from jax import lax
from jax.experimental import pallas as pl
from jax.experimental.pallas import tpu as pltpu

F32 = jnp.float32
BF16 = jnp.bfloat16

HEAD_DIM = 64
N_HEADS = 8
WIDTH = N_HEADS * HEAD_DIM
CONV_K = 3
NORM_EPS = 1e-6
NEG_INF = -1e30

LANES = 128
SUBLANES = 8
VMEM_LIMIT = 56 * 1024 * 1024

TN_IN = 512
ROW_CHUNK = 256
CUM_BLK = 256
TQ = 256
TK = 256
TM_MIX = 512
TN_UP = 256
TM_DOWN = 512
HALO = SUBLANES


def _cparams(sem):
    return pltpu.CompilerParams(dimension_semantics=sem, vmem_limit_bytes=VMEM_LIMIT)


def _log_sigmoid(v):
    return jnp.minimum(v, 0.0) - jnp.log(1.0 + jnp.exp(-jnp.abs(v)))


def _split3(v):
    hi = v.astype(BF16)
    r1 = v - hi.astype(F32)
    mid = r1.astype(BF16)
    lo = (r1 - mid.astype(F32)).astype(BF16)
    return hi, mid, lo


def _dot(a, b):
    return jnp.dot(a, b, preferred_element_type=F32)


def _dot_nt(a, b):
    return lax.dot_general(a, b, (((1,), (1,)), ((), ())), preferred_element_type=F32)


def _dot_tn(a, b):
    return lax.dot_general(a, b, (((0,), (0,)), ((), ())), preferred_element_type=F32)


J_C, J_H, J_B, J_FQ, J_FK, J_GATE = 0, 1, 2, 3, 4, 9


def _in_kernel(x_ref, g1_ref, w_ref, wf_ref, bias_ref, fb_ref, cw_ref, qkg_ref, tri_ref,
               main_ref, yc_ref, c_ref, ct_ref, hn_scr, acc_scr, ch_scr):
    j = pl.program_id(1)
    seq = x_ref.shape[0]
    n_chunks = seq // ROW_CHUNK

    @pl.when(j == 0)
    def _():
        def norm_body(r, _):
            rows = pl.ds(pl.multiple_of(r * ROW_CHUNK, ROW_CHUNK), ROW_CHUNK)
            x = x_ref[rows, :]
            ms = jnp.mean(x * x, axis=-1, keepdims=True)
            hn_scr[rows, :] = (x * lax.rsqrt(ms + NORM_EPS) * g1_ref[...]).astype(BF16)
            return 0
        lax.fori_loop(0, n_chunks, norm_body, 0)
        ch_scr[0:HALO, :] = jnp.zeros((HALO, ch_scr.shape[1]), F32)

        carry = jnp.zeros((1, LANES), F32)
        for blk in range(seq // CUM_BLK):
            rows = slice(blk * CUM_BLK, (blk + 1) * CUM_BLK)
            lf = _log_sigmoid(_dot(hn_scr[rows, :], wf_ref[...]) + fb_ref[...])
            hi, mid, lo = _split3(lf)
            tri = tri_ref[...]
            cs = _dot(tri, hi) + _dot(tri, mid) + _dot(tri, lo) + carry
            c_ref[rows, :] = cs
            carry = cs[CUM_BLK - 1:CUM_BLK, :]
        ct_ref[...] = c_ref[...].T[0:N_HEADS, :]

    acc_scr[...] = _dot(hn_scr[...], w_ref[...])

    def chunked(fn):
        def body(r, _):
            rows = pl.ds(pl.multiple_of(r * ROW_CHUNK, ROW_CHUNK), ROW_CHUNK)
            fn(rows)
            return 0
        lax.fori_loop(0, n_chunks, body, 0)

    @pl.when(j == J_C)
    def _():
        def fn(rows):
            ch_scr[pl.ds(rows.start + HALO, ROW_CHUNK), :] = acc_scr[rows, :]
        chunked(fn)

    @pl.when(j == J_H)
    def _():
        def fn(rows):
            dst = pl.ds(rows.start + HALO, ROW_CHUNK)
            ch_scr[dst, :] = ch_scr[dst, :] * acc_scr[rows, :]
        chunked(fn)

    @pl.when(j == J_B)
    def _():
        for r in range(n_chunks):
            base = r * ROW_CHUNK + HALO
            u = cw_ref[2:3, :] * ch_scr[base:base + ROW_CHUNK, :]
            u = u + cw_ref[1:2, :] * ch_scr[base - 1:base - 1 + ROW_CHUNK, :]
            u = u + cw_ref[0:1, :] * ch_scr[base - 2:base - 2 + ROW_CHUNK, :]
            rows = slice(r * ROW_CHUNK, (r + 1) * ROW_CHUNK)
            yc_ref[rows, :] = (acc_scr[rows, :] * u).astype(BF16)

    @pl.when((j == J_FQ) | (j == J_FK))
    def _():
        lane = lax.broadcasted_iota(jnp.int32, (ROW_CHUNK, LANES), 1)
        low = lane < HEAD_DIM

        def fn(rows):
            for cidx in range(TN_IN // LANES):
                cols = slice(cidx * LANES, (cidx + 1) * LANES)
                a = acc_scr[rows, cols]
                a2 = a * a
                s_lo = jnp.sum(jnp.where(low, a2, 0.0), axis=-1, keepdims=True)
                s_hi = jnp.sum(jnp.where(low, 0.0, a2), axis=-1, keepdims=True)
                ms = jnp.where(low, s_lo, s_hi) * (1.0 / HEAD_DIM)
                main_ref[rows, cols] = (a * lax.rsqrt(ms + NORM_EPS) * qkg_ref[:, cols]).astype(BF16)
        chunked(fn)

    @pl.when((j > J_FK) & (j < J_GATE))
    def _():
        def fn(rows):
            main_ref[rows, :] = acc_scr[rows, :].astype(BF16)
        chunked(fn)

    @pl.when(j >= J_GATE)
    def _():
        def fn(rows):
            main_ref[rows, :] = jax.nn.sigmoid(acc_scr[rows, :] + bias_ref[...]).astype(BF16)
        chunked(fn)


def _in_proj(x2d, g1, w_main, w_f, bias_main, f_bias, conv_w, qk_gain, tri, batch, seq):
    d_model = x2d.shape[1]
    n_main = w_main.shape[1]
    nj = n_main // TN_IN
    n_out = n_main - 3 * TN_IN
    tokens = x2d.shape[0]
    return pl.pallas_call(
        _in_kernel,
        grid=(batch, nj),
        in_specs=[
            pl.BlockSpec((seq, d_model), lambda i, j: (i, 0)),
            pl.BlockSpec((1, d_model), lambda i, j: (0, 0)),
            pl.BlockSpec((d_model, TN_IN), lambda i, j: (0, j)),
            pl.BlockSpec((d_model, LANES), lambda i, j: (0, 0)),
            pl.BlockSpec((1, TN_IN), lambda i, j: (0, j)),
            pl.BlockSpec((1, LANES), lambda i, j: (0, 0)),
            pl.BlockSpec((CONV_K, WIDTH), lambda i, j: (0, 0)),
            pl.BlockSpec((1, TN_IN), lambda i, j: (0, jnp.clip(j - J_FQ, 0, 1))),
            pl.BlockSpec((CUM_BLK, CUM_BLK), lambda i, j: (0, 0)),
        ],
        out_specs=[
            pl.BlockSpec((seq, TN_IN), lambda i, j: (i, jnp.maximum(j - J_FQ, 0))),
            pl.BlockSpec((seq, WIDTH), lambda i, j: (i, 0)),
            pl.BlockSpec((seq, LANES), lambda i, j: (i, 0)),
            pl.BlockSpec((None, N_HEADS, seq), lambda i, j: (i, 0, 0)),
        ],
        out_shape=[
            jax.ShapeDtypeStruct((tokens, n_out), BF16),
            jax.ShapeDtypeStruct((tokens, WIDTH), BF16),
            jax.ShapeDtypeStruct((tokens, LANES), F32),
            jax.ShapeDtypeStruct((batch, N_HEADS, seq), F32),
        ],
        scratch_shapes=[
            pltpu.VMEM((seq, d_model), BF16),
            pltpu.VMEM((seq, TN_IN), F32),
            pltpu.VMEM((seq + HALO, WIDTH), F32),
        ],
        compiler_params=_cparams(("arbitrary", "arbitrary")),
        name="in_proj",
    )(x2d, g1, w_main, w_f, bias_main, f_bias, conv_w, qk_gain, tri)


def _pair_masks(q_pair):
    lane = lax.broadcasted_iota(jnp.int32, q_pair.shape, 1)
    zero = jnp.zeros_like(q_pair)
    return (jnp.where(lane < HEAD_DIM, q_pair, zero), jnp.where(lane < HEAD_DIM, zero, q_pair))


def _merge_pair(o0, o1):
    return jnp.concatenate([o0[0:HEAD_DIM, :], o1[HEAD_DIM:2 * HEAD_DIM, :]], axis=0)


def _fox_kernel(q_ref, k_ref, v_ref, c_ref, ct_ref, o_ref, acc_scr):
    qi = pl.program_id(1)
    q0 = pl.multiple_of(qi * TQ, TQ)
    krow = lax.broadcasted_iota(jnp.int32, (TK, TQ), 0)
    qcol = lax.broadcasted_iota(jnp.int32, (TK, TQ), 1)
    causal = krow <= qcol

    for p in range(N_HEADS // 2):
        cols = slice(p * LANES, (p + 1) * LANES)
        qm = _pair_masks(q_ref[:, cols])
        cq = [ct_ref[2 * p + e:2 * p + e + 1, pl.ds(q0, TQ)] for e in range(2)]
        acc_scr[...] = jnp.zeros(acc_scr.shape, F32)

        def step(k0, carry, diag):
            k_blk = k_ref[pl.ds(k0, TK), cols]
            v_blk = v_ref[pl.ds(k0, TK), cols]
            out = []
            for e in range(2):
                m, l = carry[2 * e], carry[2 * e + 1]
                ck = c_ref[pl.ds(k0, TK), 2 * p + e:2 * p + e + 1]
                s = _dot_nt(k_blk, qm[e]) + (cq[e] - ck)
                if diag:
                    s = jnp.where(causal, s, NEG_INF)
                m_new = jnp.maximum(m, jnp.max(s, axis=0, keepdims=True))
                alpha = jnp.exp(m - m_new)
                pr = jnp.exp(s - m_new)
                l_new = alpha * l + jnp.sum(pr, axis=0, keepdims=True)
                acc_scr[e] = alpha * acc_scr[e] + _dot_tn(v_blk, pr.astype(BF16))
                out += [m_new, l_new]
            return tuple(out)

        init = (jnp.full((1, TQ), NEG_INF, F32), jnp.zeros((1, TQ), F32)) * 2
        carry = lax.fori_loop(
            0, qi, lambda kb, c: step(pl.multiple_of(kb * TK, TK), c, False), init)
        carry = step(q0, carry, True)
        o_pair = _merge_pair(acc_scr[0] / carry[1], acc_scr[1] / carry[3])
        o_ref[:, cols] = o_pair.T.astype(BF16)


def _sb_kernel(q_ref, k_ref, v_ref, tri_ref, o_ref, acc_scr):
    qi = pl.program_id(1)
    q0 = pl.multiple_of(qi * TQ, TQ)
    krow = lax.broadcasted_iota(jnp.int32, (TK, TQ), 0)
    qcol = lax.broadcasted_iota(jnp.int32, (TK, TQ), 1)
    strict = krow < qcol

    for p in range(N_HEADS // 2):
        cols = slice(p * LANES, (p + 1) * LANES)
        qm = _pair_masks(q_ref[:, cols])
        acc_scr[...] = jnp.zeros(acc_scr.shape, F32)

        def step(k0, carry, diag):
            k_blk = k_ref[pl.ds(k0, TK), cols]
            v_blk = v_ref[pl.ds(k0, TK), cols]
            out = []
            for e in range(2):
                z = _dot_nt(k_blk, qm[e])
                sp = jnp.log(1.0 + jnp.exp(-jnp.abs(z)))
                lsz = jnp.minimum(z, 0.0) - sp
                l1m = lsz - z
                if diag:
                    l1m = jnp.where(strict, l1m, 0.0)
                hi = l1m.astype(BF16)
                lo = (l1m - hi.astype(F32)).astype(BF16)
                tri = tri_ref[...]
                tot = lsz + _dot(tri, hi) + _dot(tri, lo) + carry[e]
                a = jnp.exp(tot)
                if diag:
                    a = jnp.where(strict, a, 0.0)
                acc_scr[e] = acc_scr[e] + _dot_tn(v_blk, a.astype(BF16))
                out.append(carry[e] + jnp.sum(l1m, axis=0, keepdims=True))
            return tuple(out)

        carry = step(q0, (jnp.zeros((1, TQ), F32),) * 2, True)
        lax.fori_loop(
            0, qi, lambda i, c: step(pl.multiple_of((qi - 1 - i) * TK, TK), c, False), carry)
        o_ref[:, cols] = _merge_pair(acc_scr[0], acc_scr[1]).T.astype(BF16)


def _fox_attention(main, c, ct, batch, seq):
    tokens = main.shape[0]
    nq = seq // TQ
    return pl.pallas_call(
        _fox_kernel,
        grid=(batch, nq),
        in_specs=[
            pl.BlockSpec((TQ, WIDTH), lambda b, qi: (b * nq + qi, 0)),
            pl.BlockSpec((seq, WIDTH), lambda b, qi: (b, 1)),
            pl.BlockSpec((seq, WIDTH), lambda b, qi: (b, 2)),
            pl.BlockSpec((seq, LANES), lambda b, qi: (b, 0)),
            pl.BlockSpec((None, N_HEADS, seq), lambda b, qi: (b, 0, 0)),
        ],
        out_specs=pl.BlockSpec((TQ, WIDTH), lambda b, qi: (b * nq + qi, 0)),
        out_shape=jax.ShapeDtypeStruct((tokens, WIDTH), BF16),
        scratch_shapes=[pltpu.VMEM((2, LANES, TQ), F32)],
        compiler_params=_cparams(("arbitrary", "arbitrary")),
        name="fox_attention",
    )(main, main, main, c, ct)


def _sb_attention(main, tri, batch, seq):
    tokens = main.shape[0]
    nq = seq // TQ
    return pl.pallas_call(
        _sb_kernel,
        grid=(batch, nq),
        in_specs=[
            pl.BlockSpec((TQ, WIDTH), lambda b, qi: (b * nq + qi, 3)),
            pl.BlockSpec((seq, WIDTH), lambda b, qi: (b, 4)),
            pl.BlockSpec((seq, WIDTH), lambda b, qi: (b, 5)),
            pl.BlockSpec((TK, TK), lambda b, qi: (0, 0)),
        ],
        out_specs=pl.BlockSpec((TQ, WIDTH), lambda b, qi: (b * nq + qi, 0)),
        out_shape=jax.ShapeDtypeStruct((tokens, WIDTH), BF16),
        scratch_shapes=[pltpu.VMEM((2, LANES, TQ), F32)],
        compiler_params=_cparams(("arbitrary", "arbitrary")),
        name="sb_attention",
    )(main, main, main, tri)


def _mix_kernel(yc_ref, af_ref, as_ref, gate_ref, x_ref, wc_ref, wf_ref, ws_ref, wo_ref, g2_ref,
                xo_ref, hn_ref):
    d_model = x_ref.shape[1]
    y = gate_ref[:, 0:d_model].astype(F32) * _dot(yc_ref[...], wc_ref[...])
    y = y + gate_ref[:, d_model:2 * d_model].astype(F32) * _dot(af_ref[...], wf_ref[...])
    y = y + gate_ref[:, 2 * d_model:3 * d_model].astype(F32) * _dot(as_ref[...], ws_ref[...])
    xn = x_ref[...] + _dot(y.astype(BF16), wo_ref[...])
    xo_ref[...] = xn
    ms = jnp.mean(xn * xn, axis=-1, keepdims=True)
    hn_ref[...] = (xn * lax.rsqrt(ms + NORM_EPS) * g2_ref[...]).astype(BF16)


def _mix(yc, af, asb, main, x2d, wc, wf, ws, wo, g2):
    tokens, d_model = x2d.shape
    gate_blk = (main.shape[1] - 3 * d_model) // (3 * d_model)
    row = lambda i: (i, 0)
    const = lambda i: (0, 0)
    return pl.pallas_call(
        _mix_kernel,
        grid=(tokens // TM_MIX,),
        in_specs=[
            pl.BlockSpec((TM_MIX, WIDTH), row),
            pl.BlockSpec((TM_MIX, WIDTH), row),
            pl.BlockSpec((TM_MIX, WIDTH), row),
            pl.BlockSpec((TM_MIX, 3 * d_model), lambda i: (i, gate_blk)),
            pl.BlockSpec((TM_MIX, d_model), row),
            pl.BlockSpec((WIDTH, d_model), const),
            pl.BlockSpec((WIDTH, d_model), const),
            pl.BlockSpec((WIDTH, d_model), const),
            pl.BlockSpec((d_model, d_model), const),
            pl.BlockSpec((1, d_model), const),
        ],
        out_specs=[pl.BlockSpec((TM_MIX, d_model), row), pl.BlockSpec((TM_MIX, d_model), row)],
        out_shape=[jax.ShapeDtypeStruct((tokens, d_model), F32),
                   jax.ShapeDtypeStruct((tokens, d_model), BF16)],
        compiler_params=_cparams(("arbitrary",)),
        name="mix",
    )(yc, af, asb, main, x2d, wc, wf, ws, wo, g2)


def _up_kernel(hn_ref, wg_ref, wv_ref, cw_ref, cb_ref, o_ref, ug_scr, uv_scr):
    seq = hn_ref.shape[0]
    ug_scr[0:HALO, :] = jnp.zeros((HALO, ug_scr.shape[1]), F32)
    ug_scr[HALO:HALO + seq, :] = _dot(hn_ref[...], wg_ref[...])
    uv_scr[...] = _dot(hn_ref[...], wv_ref[...])
    for r in range(seq // ROW_CHUNK):
        base = r * ROW_CHUNK + HALO
        u = cw_ref[2:3, :] * ug_scr[base:base + ROW_CHUNK, :] + cb_ref[...]
        u = u + cw_ref[1:2, :] * ug_scr[base - 1:base - 1 + ROW_CHUNK, :]
        u = u + cw_ref[0:1, :] * ug_scr[base - 2:base - 2 + ROW_CHUNK, :]
        rows = slice(r * ROW_CHUNK, (r + 1) * ROW_CHUNK)
        o_ref[rows, :] = (u * jax.nn.sigmoid(u) * uv_scr[rows, :]).astype(BF16)


def _ffn_up(hn2, w_up, conv_w, conv_b, batch, seq):
    tokens, d_model = hn2.shape
    d_ff = w_up.shape[1] // 2
    nj = d_ff // TN_UP
    return pl.pallas_call(
        _up_kernel,
        grid=(batch, nj),
        in_specs=[
            pl.BlockSpec((seq, d_model), lambda b, j: (b, 0)),
            pl.BlockSpec((d_model, TN_UP), lambda b, j: (0, j)),
            pl.BlockSpec((d_model, TN_UP), lambda b, j: (0, nj + j)),
            pl.BlockSpec((CONV_K, TN_UP), lambda b, j: (0, j)),
            pl.BlockSpec((1, TN_UP), lambda b, j: (0, j)),
        ],
        out_specs=pl.BlockSpec((seq, TN_UP), lambda b, j: (b, j)),
        out_shape=jax.ShapeDtypeStruct((tokens, d_ff), BF16),
        scratch_shapes=[pltpu.VMEM((seq + HALO, TN_UP), F32), pltpu.VMEM((seq, TN_UP), F32)],
        compiler_params=_cparams(("arbitrary", "arbitrary")),
        name="ffn_up",
    )(hn2, w_up, w_up, conv_w, conv_b)


def _down_kernel(h_ref, w_ref, x_ref, o_ref):
    o_ref[...] = x_ref[...] + _dot(h_ref[...], w_ref[...])


def _ffn_down(hff, w_down, x2d):
    tokens, d_model = x2d.shape
    d_ff = hff.shape[1]
    return pl.pallas_call(
        _down_kernel,
        grid=(tokens // TM_DOWN,),
        in_specs=[
            pl.BlockSpec((TM_DOWN, d_ff), lambda i: (i, 0)),
            pl.BlockSpec((d_ff, d_model), lambda i: (0, 0)),
            pl.BlockSpec((TM_DOWN, d_model), lambda i: (i, 0)),
        ],
        out_specs=pl.BlockSpec((TM_DOWN, d_model), lambda i: (i, 0)),
        out_shape=jax.ShapeDtypeStruct((tokens, d_model), F32),
        compiler_params=_cparams(("arbitrary",)),
        name="ffn_down",
    )(hff, w_down, x2d)


def kernel(x, norm1_g, w_in, fox_f_bias, gate_bias, conv_w, fox_q_norm_g, fox_k_norm_g, w_proj_conv,
           w_proj_fox, w_proj_sb, w_out, norm2_g, w_up, ffn_conv_w, ffn_conv_b, w_down):
    batch, seq, d_model = x.shape
    depth = w_in.shape[0]
    scale = HEAD_DIM ** -0.5
    assert seq % max(TQ, TK, CUM_BLK, ROW_CHUNK) == 0 and TQ == TK

    o_f = 3 * WIDTH + 3 * WIDTH
    o_sb = o_f + N_HEADS
    o_gate = o_sb + 3 * WIDTH
    w_main = jnp.concatenate([
        w_in[:, :, WIDTH:2 * WIDTH], w_in[:, :, 2 * WIDTH:3 * WIDTH], w_in[:, :, 0:WIDTH],
        w_in[:, :, 3 * WIDTH:o_f],
        w_in[:, :, o_sb:o_sb + WIDTH] * scale, w_in[:, :, o_sb + WIDTH:o_gate],
        w_in[:, :, o_gate:]], axis=-1).astype(BF16)
    w_f = jnp.pad(w_in[:, :, o_f:o_sb], ((0, 0), (0, 0), (0, LANES - N_HEADS))).astype(BF16)
    n_main = w_main.shape[-1]
    bias_main = jnp.pad(gate_bias, ((0, 0), (n_main - gate_bias.shape[1], 0)))[:, None, :]
    f_bias = jnp.pad(fox_f_bias, ((0, 0), (0, LANES - N_HEADS)))[:, None, :]
    qk_gain = jnp.concatenate([jnp.tile(fox_q_norm_g * scale, (1, N_HEADS)),
                               jnp.tile(fox_k_norm_g, (1, N_HEADS))], axis=-1)[:, None, :]
    wc, wf, ws, wo = (w.astype(BF16) for w in (w_proj_conv, w_proj_fox, w_proj_sb, w_out))
    w_up_b, w_down_b = w_up.astype(BF16), w_down.astype(BF16)

    idx = jnp.arange(CUM_BLK)
    tri_incl = (idx[None, :] <= idx[:, None]).astype(BF16)
    tri_later = (idx[None, :] > idx[:, None]).astype(BF16)

    x2d = x.reshape(batch * seq, d_model)
    for l in range(depth):
        main, yc, c, ct = _in_proj(x2d, norm1_g[l][None, :], w_main[l], w_f[l], bias_main[l], f_bias[l],
                                   conv_w[l], qk_gain[l], tri_incl, batch, seq)
        af = _fox_attention(main, c, ct, batch, seq)
        asb = _sb_attention(main, tri_later, batch, seq)
        x2d, hn2 = _mix(yc, af, asb, main, x2d, wc[l], wf[l], ws[l], wo[l], norm2_g[l][None, :])
        hff = _ffn_up(hn2, w_up_b[l], ffn_conv_w[l], ffn_conv_b[l][None, :], batch, seq)
        x2d = _ffn_down(hff, w_down_b[l], x2d)
    return x2d.reshape(batch, seq, d_model)
```

```python
import functools

import jax
import jax.numpy as jnp
from jax import lax
from jax.experimental import pallas as pl
from jax.experimental.pallas import tpu as pltpu

F32 = jnp.float32
BF16 = jnp.bfloat16

HEAD_DIM = 64
N_HEADS = 8
WIDTH = N_HEADS * HEAD_DIM
CONV_K = 3
NORM_EPS = 1e-6
NEG_INF = -1e30
LOG2E = 1.4426950408889634

LANES = 128
SUBLANES = 8
VMEM_LIMIT = 56 * 1024 * 1024

TN_IN = 512
ROW_CHUNK = 256
CUM_BLK = 256
TQ = 256
TK = 256
TM_MIX = 512
TN_UP = 256
TM_DOWN = 512
HALO = SUBLANES


def _cparams(sem):
    return pltpu.CompilerParams(dimension_semantics=sem, vmem_limit_bytes=VMEM_LIMIT)


def _log_sigmoid(v):
    return jnp.minimum(v, 0.0) - jnp.log(1.0 + jnp.exp(-jnp.abs(v)))


def _split3(v):
    hi = v.astype(BF16)
    r1 = v - hi.astype(F32)
    mid = r1.astype(BF16)
    lo = (r1 - mid.astype(F32)).astype(BF16)
    return hi, mid, lo


def _dot(a, b):
    return jnp.dot(a, b, preferred_element_type=F32)


def _dot_nt(a, b):
    return lax.dot_general(a, b, (((1,), (1,)), ((), ())), preferred_element_type=F32)


def _dot_tn(a, b):
    return lax.dot_general(a, b, (((0,), (0,)), ((), ())), preferred_element_type=F32)


J_C, J_H, J_B, J_FQ, J_FK, J_GATE = 0, 1, 2, 3, 4, 9


def _in_kernel(x_ref, g1_ref, w_ref, wf_ref, bias_ref, fb_ref, cw_ref, qkg_ref, tri_ref,
               main_ref, yc_ref, c_ref, hn_scr, acc_scr, ch_scr):
    j = pl.program_id(1)
    seq = x_ref.shape[0]
    n_chunks = seq // ROW_CHUNK

    @pl.when(j == 0)
    def _():
        def norm_body(r, _):
            rows = pl.ds(pl.multiple_of(r * ROW_CHUNK, ROW_CHUNK), ROW_CHUNK)
            x = x_ref[rows, :]
            ms = jnp.mean(x * x, axis=-1, keepdims=True)
            hn_scr[rows, :] = (x * lax.rsqrt(ms + NORM_EPS) * g1_ref[...]).astype(BF16)
            return 0
        lax.fori_loop(0, n_chunks, norm_body, 0)
        ch_scr[0:HALO, :] = jnp.zeros((HALO, ch_scr.shape[1]), F32)

        carry = jnp.zeros((1, LANES), F32)
        for blk in range(seq // CUM_BLK):
            rows = slice(blk * CUM_BLK, (blk + 1) * CUM_BLK)
            lf = _log_sigmoid(_dot(hn_scr[rows, :], wf_ref[...]) + fb_ref[...]) * LOG2E
            hi, mid, lo = _split3(lf)
            tri = tri_ref[...]
            cs = _dot(tri, hi) + _dot(tri, mid) + _dot(tri, lo) + carry
            c_ref[rows, :] = cs
            carry = cs[CUM_BLK - 1:CUM_BLK, :]

    acc_scr[...] = _dot(hn_scr[...], w_ref[...])

    def chunked(fn):
        def body(r, _):
            rows = pl.ds(pl.multiple_of(r * ROW_CHUNK, ROW_CHUNK), ROW_CHUNK)
            fn(rows)
            return 0
        lax.fori_loop(0, n_chunks, body, 0)

    @pl.when(j == J_C)
    def _():
        def fn(rows):
            ch_scr[pl.ds(rows.start + HALO, ROW_CHUNK), :] = acc_scr[rows, :]
        chunked(fn)

    @pl.when(j == J_H)
    def _():
        def fn(rows):
            dst = pl.ds(rows.start + HALO, ROW_CHUNK)
            ch_scr[dst, :] = ch_scr[dst, :] * acc_scr[rows, :]
        chunked(fn)

    @pl.when(j == J_B)
    def _():
        for r in range(n_chunks):
            base = r * ROW_CHUNK + HALO
            u = cw_ref[2:3, :] * ch_scr[base:base + ROW_CHUNK, :]
            u = u + cw_ref[1:2, :] * ch_scr[base - 1:base - 1 + ROW_CHUNK, :]
            u = u + cw_ref[0:1, :] * ch_scr[base - 2:base - 2 + ROW_CHUNK, :]
            rows = slice(r * ROW_CHUNK, (r + 1) * ROW_CHUNK)
            yc_ref[rows, :] = (acc_scr[rows, :] * u).astype(BF16)

    @pl.when((j == J_FQ) | (j == J_FK))
    def _():
        lane = lax.broadcasted_iota(jnp.int32, (ROW_CHUNK, LANES), 1)
        low = lane < HEAD_DIM

        def fn(rows):
            for cidx in range(TN_IN // LANES):
                cols = slice(cidx * LANES, (cidx + 1) * LANES)
                a = acc_scr[rows, cols]
                a2 = a * a
                s_lo = jnp.sum(jnp.where(low, a2, 0.0), axis=-1, keepdims=True)
                s_hi = jnp.sum(jnp.where(low, 0.0, a2), axis=-1, keepdims=True)
                ms = jnp.where(low, s_lo, s_hi) * (1.0 / HEAD_DIM)
                main_ref[rows, cols] = (a * lax.rsqrt(ms + NORM_EPS) * qkg_ref[:, cols]).astype(BF16)
        chunked(fn)

    @pl.when((j > J_FK) & (j < J_GATE))
    def _():
        def fn(rows):
            main_ref[rows, :] = acc_scr[rows, :].astype(BF16)
        chunked(fn)

    @pl.when(j >= J_GATE)
    def _():
        def fn(rows):
            main_ref[rows, :] = jax.nn.sigmoid(acc_scr[rows, :] + bias_ref[...]).astype(BF16)
        chunked(fn)


def _in_proj(x2d, g1, w_main, w_f, bias_main, f_bias, conv_w, qk_gain, tri, batch, seq):
    d_model = x2d.shape[1]
    n_main = w_main.shape[1]
    nj = n_main // TN_IN
    n_out = n_main - 3 * TN_IN
    tokens = x2d.shape[0]
    return pl.pallas_call(
        _in_kernel,
        grid=(batch, nj),
        in_specs=[
            pl.BlockSpec((seq, d_model), lambda i, j: (i, 0)),
            pl.BlockSpec((1, d_model), lambda i, j: (0, 0)),
            pl.BlockSpec((d_model, TN_IN), lambda i, j: (0, j)),
            pl.BlockSpec((d_model, LANES), lambda i, j: (0, 0)),
            pl.BlockSpec((1, TN_IN), lambda i, j: (0, j)),
            pl.BlockSpec((1, LANES), lambda i, j: (0, 0)),
            pl.BlockSpec((CONV_K, WIDTH), lambda i, j: (0, 0)),
            pl.BlockSpec((1, TN_IN), lambda i, j: (0, jnp.clip(j - J_FQ, 0, 1))),
            pl.BlockSpec((CUM_BLK, CUM_BLK), lambda i, j: (0, 0)),
        ],
        out_specs=[
            pl.BlockSpec((seq, TN_IN), lambda i, j: (i, jnp.maximum(j - J_FQ, 0))),
            pl.BlockSpec((seq, WIDTH), lambda i, j: (i, 0)),
            pl.BlockSpec((seq, LANES), lambda i, j: (i, 0)),
        ],
        out_shape=[
            jax.ShapeDtypeStruct((tokens, n_out), BF16),
            jax.ShapeDtypeStruct((tokens, WIDTH), BF16),
            jax.ShapeDtypeStruct((tokens, LANES), F32),
        ],
        scratch_shapes=[
            pltpu.VMEM((seq, d_model), BF16),
            pltpu.VMEM((seq, TN_IN), F32),
            pltpu.VMEM((seq + HALO, WIDTH), F32),
        ],
        compiler_params=_cparams(("arbitrary", "arbitrary")),
        name="in_proj",
    )(x2d, g1, w_main, w_f, bias_main, f_bias, conv_w, qk_gain, tri)


def _pair_masks(q_pair):
    lane = lax.broadcasted_iota(jnp.int32, q_pair.shape, 1)
    zero = jnp.zeros_like(q_pair)
    return (jnp.where(lane < HEAD_DIM, q_pair, zero), jnp.where(lane < HEAD_DIM, zero, q_pair))


def _merge_pair(o0, o1):
    return jnp.concatenate([o0[0:HEAD_DIM, :], o1[HEAD_DIM:2 * HEAD_DIM, :]], axis=0)


def _store_masked_queries(q_ref, qm_scr):
    for p in range(N_HEADS // 2):
        qm = _pair_masks(q_ref[:, p * LANES:(p + 1) * LANES])
        qm_scr[2 * p] = qm[0]
        qm_scr[2 * p + 1] = qm[1]


def _fox_kernel(q_ref, k_ref, v_ref, c_ref, o_ref, qm_scr, acc_scr, m_scr, l_scr):
    qi = pl.program_id(1)
    q0 = pl.multiple_of(qi * TQ, TQ)
    krow = lax.broadcasted_iota(jnp.int32, (TK, TQ), 0)
    qcol = lax.broadcasted_iota(jnp.int32, (TK, TQ), 1)
    causal = krow <= qcol

    _store_masked_queries(q_ref, qm_scr)
    acc_scr[...] = jnp.zeros(acc_scr.shape, F32)
    m_scr[...] = jnp.full(m_scr.shape, NEG_INF, F32)
    l_scr[...] = jnp.zeros(l_scr.shape, F32)

    def step(k0, diag):
        heads = range(N_HEADS)
        pair_cols = lambda h: slice((h // 2) * LANES, (h // 2 + 1) * LANES)
        s_all = [_dot_nt(k_ref[pl.ds(k0, TK), pair_cols(h)], qm_scr[h]) for h in heads]
        pr_all, alpha_all = [], []
        for h in heads:
            s = s_all[h] - c_ref[pl.ds(k0, TK), h:h + 1]
            if diag:
                s = jnp.where(causal, s, NEG_INF)
            m = m_scr[h:h + 1, :]
            m_new = jnp.maximum(m, jnp.max(s, axis=0, keepdims=True))
            alpha = jnp.exp2(m - m_new)
            pr = jnp.exp2(s - m_new)
            m_scr[h:h + 1, :] = m_new
            l_scr[h:h + 1, :] = alpha * l_scr[h:h + 1, :] + jnp.sum(pr, axis=0, keepdims=True)
            pr_all.append(pr.astype(BF16))
            alpha_all.append(alpha)
        for h in heads:
            pv = _dot_tn(v_ref[pl.ds(k0, TK), pair_cols(h)], pr_all[h])
            acc_scr[h] = alpha_all[h] * acc_scr[h] + pv

    def body(kb, _):
        step(pl.multiple_of(kb * TK, TK), False)
        return 0
    lax.fori_loop(0, qi, body, 0)
    step(q0, True)
    for p in range(N_HEADS // 2):
        o_pair = _merge_pair(acc_scr[2 * p] / l_scr[2 * p:2 * p + 1, :],
                             acc_scr[2 * p + 1] / l_scr[2 * p + 1:2 * p + 2, :])
        o_ref[:, p * LANES:(p + 1) * LANES] = o_pair.T.astype(BF16)


def _sb_kernel(q_ref, k_ref, v_ref, tri_ref, o_ref, qm_scr, acc_scr, r_scr):
    qi = pl.program_id(1)
    q0 = pl.multiple_of(qi * TQ, TQ)
    krow = lax.broadcasted_iota(jnp.int32, (TK, TQ), 0)
    qcol = lax.broadcasted_iota(jnp.int32, (TK, TQ), 1)
    strict = krow < qcol
    last_row = lax.broadcasted_iota(jnp.int32, (SUBLANES, TQ), 0) == SUBLANES - 1

    _store_masked_queries(q_ref, qm_scr)
    acc_scr[...] = jnp.zeros(acc_scr.shape, F32)
    r_scr[...] = jnp.zeros(r_scr.shape, F32)

    def step(k0, diag):
        heads = range(N_HEADS)
        pair_cols = lambda h: slice((h // 2) * LANES, (h // 2 + 1) * LANES)
        z_all = [_dot_nt(k_ref[pl.ds(k0, TK), pair_cols(h)], qm_scr[h]) for h in heads]
        la_all = []
        for h in heads:
            z = z_all[h]
            neg_abs = lax.bitcast_convert_type(
                lax.bitcast_convert_type(z, jnp.uint32) | jnp.uint32(0x80000000), F32)
            u = jnp.maximum(z, 0.0) + jnp.log(1.0 + jnp.exp2(neg_abs)) * LOG2E
            if diag:
                u = jnp.where(strict, u, 0.0)
            else:
                tail = u[TK - SUBLANES:, :] + jnp.where(last_row, r_scr[h:h + 1, :], 0.0)
                u = jnp.concatenate([u[:TK - SUBLANES, :], tail], axis=0)
            hi = u.astype(BF16)
            lo = (u - hi.astype(F32)).astype(BF16)
            tri = tri_ref[...]
            cs = _dot(tri, hi) + _dot(tri, lo)
            r_scr[h:h + 1, :] = cs[0:1, :]
            la_all.append(z - cs)
        for h in heads:
            a = jnp.exp2(la_all[h])
            if diag:
                a = jnp.where(strict, a, 0.0)
            acc_scr[h] = acc_scr[h] + _dot_tn(v_ref[pl.ds(k0, TK), pair_cols(h)], a.astype(BF16))

    step(q0, True)

    def body(i, _):
        step(pl.multiple_of((qi - 1 - i) * TK, TK), False)
        return 0
    lax.fori_loop(0, qi, body, 0)
    for p in range(N_HEADS // 2):
        o_pair = _merge_pair(acc_scr[2 * p], acc_scr[2 * p + 1])
        o_ref[:, p * LANES:(p + 1) * LANES] = o_pair.T.astype(BF16)


def _fox_attention(main, c, batch, seq):
    tokens = main.shape[0]
    nq = seq // TQ
    return pl.pallas_call(
        _fox_kernel,
        grid=(batch, nq),
        in_specs=[
            pl.BlockSpec((TQ, WIDTH), lambda b, qi: (b * nq + qi, 0)),
            pl.BlockSpec((seq, WIDTH), lambda b, qi: (b, 1)),
            pl.BlockSpec((seq, WIDTH), lambda b, qi: (b, 2)),
            pl.BlockSpec((seq, LANES), lambda b, qi: (b, 0)),
        ],
        out_specs=pl.BlockSpec((TQ, WIDTH), lambda b, qi: (b * nq + qi, 0)),
        out_shape=jax.ShapeDtypeStruct((tokens, WIDTH), BF16),
        scratch_shapes=[pltpu.VMEM((N_HEADS, TQ, LANES), BF16), pltpu.VMEM((N_HEADS, LANES, TQ), F32),
                        pltpu.VMEM((N_HEADS, TQ), F32), pltpu.VMEM((N_HEADS, TQ), F32)],
        compiler_params=_cparams(("arbitrary", "arbitrary")),
        name="fox_attention",
    )(main, main, main, c)


def _sb_attention(main, tri, batch, seq):
    tokens = main.shape[0]
    nq = seq // TQ
    return pl.pallas_call(
        _sb_kernel,
        grid=(batch, nq),
        in_specs=[
            pl.BlockSpec((TQ, WIDTH), lambda b, qi: (b * nq + qi, 3)),
            pl.BlockSpec((seq, WIDTH), lambda b, qi: (b, 4)),
            pl.BlockSpec((seq, WIDTH), lambda b, qi: (b, 5)),
            pl.BlockSpec((TK, TK), lambda b, qi: (0, 0)),
        ],
        out_specs=pl.BlockSpec((TQ, WIDTH), lambda b, qi: (b * nq + qi, 0)),
        out_shape=jax.ShapeDtypeStruct((tokens, WIDTH), BF16),
        scratch_shapes=[pltpu.VMEM((N_HEADS, TQ, LANES), BF16), pltpu.VMEM((N_HEADS, LANES, TQ), F32),
                        pltpu.VMEM((N_HEADS, TQ), F32)],
        compiler_params=_cparams(("arbitrary", "arbitrary")),
        name="sb_attention",
    )(main, main, main, tri)


def _mix_kernel(yc_ref, af_ref, as_ref, gate_ref, x_ref, wc_ref, wf_ref, ws_ref, wo_ref, g2_ref,
                xo_ref, hn_ref):
    d_model = x_ref.shape[1]
    y = gate_ref[:, 0:d_model].astype(F32) * _dot(yc_ref[...], wc_ref[...])
    y = y + gate_ref[:, d_model:2 * d_model].astype(F32) * _dot(af_ref[...], wf_ref[...])
    y = y + gate_ref[:, 2 * d_model:3 * d_model].astype(F32) * _dot(as_ref[...], ws_ref[...])
    xn = x_ref[...] + _dot(y.astype(BF16), wo_ref[...])
    xo_ref[...] = xn
    ms = jnp.mean(xn * xn, axis=-1, keepdims=True)
    hn_ref[...] = (xn * lax.rsqrt(ms + NORM_EPS) * g2_ref[...]).astype(BF16)


def _mix(yc, af, asb, main, x2d, wc, wf, ws, wo, g2):
    tokens, d_model = x2d.shape
    gate_blk = (main.shape[1] - 3 * d_model) // (3 * d_model)
    row = lambda i: (i, 0)
    const = lambda i: (0, 0)
    return pl.pallas_call(
        _mix_kernel,
        grid=(tokens // TM_MIX,),
        in_specs=[
            pl.BlockSpec((TM_MIX, WIDTH), row),
            pl.BlockSpec((TM_MIX, WIDTH), row),
            pl.BlockSpec((TM_MIX, WIDTH), row),
            pl.BlockSpec((TM_MIX, 3 * d_model), lambda i: (i, gate_blk)),
            pl.BlockSpec((TM_MIX, d_model), row),
            pl.BlockSpec((WIDTH, d_model), const),
            pl.BlockSpec((WIDTH, d_model), const),
            pl.BlockSpec((WIDTH, d_model), const),
            pl.BlockSpec((d_model, d_model), const),
            pl.BlockSpec((1, d_model), const),
        ],
        out_specs=[pl.BlockSpec((TM_MIX, d_model), row), pl.BlockSpec((TM_MIX, d_model), row)],
        out_shape=[jax.ShapeDtypeStruct((tokens, d_model), F32),
                   jax.ShapeDtypeStruct((tokens, d_model), BF16)],
        compiler_params=_cparams(("arbitrary",)),
        name="mix",
    )(yc, af, asb, main, x2d, wc, wf, ws, wo, g2)


def _up_kernel(hn_ref, wg_ref, wv_ref, cw_ref, cb_ref, o_ref, ug_scr, uv_scr):
    seq = hn_ref.shape[0]
    ug_scr[0:HALO, :] = jnp.zeros((HALO, ug_scr.shape[1]), F32)
    ug_scr[HALO:HALO + seq, :] = _dot(hn_ref[...], wg_ref[...])
    uv_scr[...] = _dot(hn_ref[...], wv_ref[...])
    for r in range(seq // ROW_CHUNK):
        base = r * ROW_CHUNK + HALO
        u = cw_ref[2:3, :] * ug_scr[base:base + ROW_CHUNK, :] + cb_ref[...]
        u = u + cw_ref[1:2, :] * ug_scr[base - 1:base - 1 + ROW_CHUNK, :]
        u = u + cw_ref[0:1, :] * ug_scr[base - 2:base - 2 + ROW_CHUNK, :]
        rows = slice(r * ROW_CHUNK, (r + 1) * ROW_CHUNK)
        o_ref[rows, :] = (u * jax.nn.sigmoid(u) * uv_scr[rows, :]).astype(BF16)


def _ffn_up(hn2, w_up, conv_w, conv_b, batch, seq):
    tokens, d_model = hn2.shape
    d_ff = w_up.shape[1] // 2
    nj = d_ff // TN_UP
    return pl.pallas_call(
        _up_kernel,
        grid=(batch, nj),
        in_specs=[
            pl.BlockSpec((seq, d_model), lambda b, j: (b, 0)),
            pl.BlockSpec((d_model, TN_UP), lambda b, j: (0, j)),
            pl.BlockSpec((d_model, TN_UP), lambda b, j: (0, nj + j)),
            pl.BlockSpec((CONV_K, TN_UP), lambda b, j: (0, j)),
            pl.BlockSpec((1, TN_UP), lambda b, j: (0, j)),
        ],
        out_specs=pl.BlockSpec((seq, TN_UP), lambda b, j: (b, j)),
        out_shape=jax.ShapeDtypeStruct((tokens, d_ff), BF16),
        scratch_shapes=[pltpu.VMEM((seq + HALO, TN_UP), F32), pltpu.VMEM((seq, TN_UP), F32)],
        compiler_params=_cparams(("arbitrary", "arbitrary")),
        name="ffn_up",
    )(hn2, w_up, w_up, conv_w, conv_b)


def _down_kernel(h_ref, w_ref, x_ref, o_ref):
    o_ref[...] = x_ref[...] + _dot(h_ref[...], w_ref[...])


def _ffn_down(hff, w_down, x2d):
    tokens, d_model = x2d.shape
    d_ff = hff.shape[1]
    return pl.pallas_call(
        _down_kernel,
        grid=(tokens // TM_DOWN,),
        in_specs=[
            pl.BlockSpec((TM_DOWN, d_ff), lambda i: (i, 0)),
            pl.BlockSpec((d_ff, d_model), lambda i: (0, 0)),
            pl.BlockSpec((TM_DOWN, d_model), lambda i: (i, 0)),
        ],
        out_specs=pl.BlockSpec((TM_DOWN, d_model), lambda i: (i, 0)),
        out_shape=jax.ShapeDtypeStruct((tokens, d_model), F32),
        compiler_params=_cparams(("arbitrary",)),
        name="ffn_down",
    )(hff, w_down, x2d)


def kernel(x, norm1_g, w_in, fox_f_bias, gate_bias, conv_w, fox_q_norm_g, fox_k_norm_g, w_proj_conv,
           w_proj_fox, w_proj_sb, w_out, norm2_g, w_up, ffn_conv_w, ffn_conv_b, w_down):
    batch, seq, d_model = x.shape
    depth = w_in.shape[0]
    scale = HEAD_DIM ** -0.5 * LOG2E
    assert seq % max(TQ, TK, CUM_BLK, ROW_CHUNK) == 0 and TQ == TK

    o_f = 3 * WIDTH + 3 * WIDTH
    o_sb = o_f + N_HEADS
    o_gate = o_sb + 3 * WIDTH
    w_main = jnp.concatenate([
        w_in[:, :, WIDTH:2 * WIDTH], w_in[:, :, 2 * WIDTH:3 * WIDTH], w_in[:, :, 0:WIDTH],
        w_in[:, :, 3 * WIDTH:o_f],
        w_in[:, :, o_sb:o_sb + WIDTH] * scale, w_in[:, :, o_sb + WIDTH:o_gate],
        w_in[:, :, o_gate:]], axis=-1).astype(BF16)
    w_f = jnp.pad(w_in[:, :, o_f:o_sb], ((0, 0), (0, 0), (0, LANES - N_HEADS))).astype(BF16)
    n_main = w_main.shape[-1]
    bias_main = jnp.pad(gate_bias, ((0, 0), (n_main - gate_bias.shape[1], 0)))[:, None, :]
    f_bias = jnp.pad(fox_f_bias, ((0, 0), (0, LANES - N_HEADS)))[:, None, :]
    qk_gain = jnp.concatenate([jnp.tile(fox_q_norm_g * scale, (1, N_HEADS)),
                               jnp.tile(fox_k_norm_g, (1, N_HEADS))], axis=-1)[:, None, :]
    wc, wf, ws, wo = (w.astype(BF16) for w in (w_proj_conv, w_proj_fox, w_proj_sb, w_out))
    w_up_b, w_down_b = w_up.astype(BF16), w_down.astype(BF16)

    idx = jnp.arange(CUM_BLK)
    tri_incl = (idx[None, :] <= idx[:, None]).astype(BF16)
    tri_suffix = (idx[None, :] >= idx[:, None]).astype(BF16)

    x2d = x.reshape(batch * seq, d_model)
    for l in range(depth):
        main, yc, c = _in_proj(x2d, norm1_g[l][None, :], w_main[l], w_f[l], bias_main[l], f_bias[l],
                               conv_w[l], qk_gain[l], tri_incl, batch, seq)
        af = _fox_attention(main, c, batch, seq)
        asb = _sb_attention(main, tri_suffix, batch, seq)
        x2d, hn2 = _mix(yc, af, asb, main, x2d, wc[l], wf[l], ws[l], wo[l], norm2_g[l][None, :])
        hff = _ffn_up(hn2, w_up_b[l], ffn_conv_w[l], ffn_conv_b[l][None, :], batch, seq)
        x2d = _ffn_down(hff, w_down_b[l], x2d)
    return x2d.reshape(batch, seq, d_model)
```

```python
import functools

import jax
import jax.numpy as jnp
from jax import lax
from jax.experimental import pallas as pl
from jax.experimental.pallas import tpu as pltpu

F32 = jnp.float32
BF16 = jnp.bfloat16

HEAD_DIM = 64
N_HEADS = 8
WIDTH = N_HEADS * HEAD_DIM
CONV_K = 3
NORM_EPS = 1e-6
NEG_INF = -1e30
LOG2E = 1.4426950408889634

LANES = 128
SUBLANES = 8
VMEM_LIMIT = 56 * 1024 * 1024

TN_IN = 512
ROW_CHUNK = 256
MM_ROWS = 512
CUM_BLK = 256
TQ = 256
TK = 256
TM_MIX = 512
TN_UP = 256
TM_DOWN = 512
HALO = SUBLANES


def _cparams(sem):
    return pltpu.CompilerParams(dimension_semantics=sem, vmem_limit_bytes=VMEM_LIMIT)


def _log_sigmoid(v):
    return jnp.minimum(v, 0.0) - jnp.log(1.0 + jnp.exp(-jnp.abs(v)))


def _split3(v):
    hi = v.astype(BF16)
    r1 = v - hi.astype(F32)
    mid = r1.astype(BF16)
    lo = (r1 - mid.astype(F32)).astype(BF16)
    return hi, mid, lo


def _dot(a, b):
    return jnp.dot(a, b, preferred_element_type=F32)


def _dot_nt(a, b):
    return lax.dot_general(a, b, (((1,), (1,)), ((), ())), preferred_element_type=F32)


def _dot_tn(a, b):
    return lax.dot_general(a, b, (((0,), (0,)), ((), ())), preferred_element_type=F32)


J_C, J_H, J_B, J_FQ, J_FK, J_FV, J_SQ, J_GATE = 0, 1, 2, 3, 4, 5, 6, 9


def _first_view_tile(j):
    return jnp.where(j < J_FQ, (j + 1) % 3, jnp.minimum(j, J_FV))


def _sigmoid(v):
    return 0.5 * jnp.tanh(0.5 * v) + 0.5


def _in_kernel(x_ref, g1_ref, wa_ref, wb_ref, wf_ref, bias_ref, fb_ref, cw_ref, qkg_ref, tri_ref,
               main_ref, yc_ref, c_ref, hn_scr, ch_scr, *, sb_scale):
    j = pl.program_id(1)
    seq = x_ref.shape[0]

    def tile(epilogue, w_ref):
        for r in range(seq // MM_ROWS):
            rows = slice(r * MM_ROWS, (r + 1) * MM_ROWS)
            epilogue(r, rows, _dot(hn_scr[rows, :], w_ref[...]))

    def store_c(r, rows, acc):
        ch_scr[HALO + r * MM_ROWS:HALO + (r + 1) * MM_ROWS, :] = acc

    @pl.when(j == J_C)
    def _():
        def norm_body(r, _):
            rows = pl.ds(pl.multiple_of(r * ROW_CHUNK, ROW_CHUNK), ROW_CHUNK)
            x = x_ref[rows, :]
            ms = jnp.mean(x * x, axis=-1, keepdims=True)
            hn_scr[rows, :] = (x * lax.rsqrt(ms + NORM_EPS) * g1_ref[...]).astype(BF16)
            return 0
        lax.fori_loop(0, seq // ROW_CHUNK, norm_body, 0)
        ch_scr[0:HALO, :] = jnp.zeros((HALO, ch_scr.shape[1]), F32)
        tile(store_c, wa_ref)

        carry = jnp.zeros((1, LANES), F32)
        for blk in range(seq // CUM_BLK):
            rows = slice(blk * CUM_BLK, (blk + 1) * CUM_BLK)
            lf = _log_sigmoid(_dot(hn_scr[rows, :], wf_ref[...]) + fb_ref[...]) * LOG2E
            hi, mid, lo = _split3(lf)
            tri = tri_ref[...]
            cs = _dot(tri, hi) + _dot(tri, mid) + _dot(tri, lo) + carry
            c_ref[rows, :] = cs
            carry = cs[CUM_BLK - 1:CUM_BLK, :]

    @pl.when(j == J_H)
    def _():
        def fn(r, rows, acc):
            dst = slice(HALO + r * MM_ROWS, HALO + (r + 1) * MM_ROWS)
            ch_scr[dst, :] = ch_scr[dst, :] * acc
        tile(fn, wa_ref)

    @pl.when(j == J_B)
    def _():
        def fn(r, rows, acc):
            base = r * MM_ROWS + HALO
            u = cw_ref[2:3, :] * ch_scr[base:base + MM_ROWS, :]
            u = u + cw_ref[1:2, :] * ch_scr[base - 1:base - 1 + MM_ROWS, :]
            u = u + cw_ref[0:1, :] * ch_scr[base - 2:base - 2 + MM_ROWS, :]
            yc_ref[rows, :] = (acc * u).astype(BF16)
        tile(fn, wa_ref)

    @pl.when((j == J_FQ) | (j == J_FK))
    def _():
        lane = lax.broadcasted_iota(jnp.int32, (MM_ROWS, LANES), 1)
        low = lane < HEAD_DIM

        def fn(r, rows, acc):
            for cidx in range(TN_IN // LANES):
                cols = slice(cidx * LANES, (cidx + 1) * LANES)
                a = acc[:, cols]
                a2 = a * a
                s_lo = jnp.sum(jnp.where(low, a2, 0.0), axis=-1, keepdims=True)
                s_hi = jnp.sum(jnp.where(low, 0.0, a2), axis=-1, keepdims=True)
                ms = jnp.where(low, s_lo, s_hi) * (1.0 / HEAD_DIM)
                main_ref[rows, cols] = (a * lax.rsqrt(ms + NORM_EPS) * qkg_ref[:, cols]).astype(BF16)
        tile(fn, wa_ref)

    def store_raw(r, rows, acc):
        main_ref[rows, :] = acc.astype(BF16)

    @pl.when(j == J_FV)
    def _():
        tile(store_raw, wa_ref)

    @pl.when(j == J_SQ)
    def _():
        def fn(r, rows, acc):
            main_ref[rows, :] = (acc * sb_scale).astype(BF16)
        tile(fn, wb_ref)

    @pl.when((j > J_SQ) & (j < J_GATE))
    def _():
        tile(store_raw, wb_ref)

    @pl.when(j >= J_GATE)
    def _():
        def fn(r, rows, acc):
            main_ref[rows, :] = _sigmoid(acc + bias_ref[...]).astype(BF16)
        tile(fn, wb_ref)


def _in_proj(layer, x2d, g1, w_a, w_b, w_f, gate_bias, f_bias, conv_w, qk_gain, tri, batch, seq, sb_scale):
    d_model = x2d.shape[1]
    assert w_a.shape[2] == (J_FV + 1) * TN_IN and gate_bias.shape[1] % TN_IN == 0
    nj = (w_a.shape[2] + w_b.shape[2]) // TN_IN
    n_out = (nj - J_FQ) * TN_IN
    tokens = x2d.shape[0]
    return pl.pallas_call(
        functools.partial(_in_kernel, sb_scale=sb_scale),
        grid=(batch, nj),
        in_specs=[
            pl.BlockSpec((seq, d_model), lambda i, j: (i, 0)),
            pl.BlockSpec((1, d_model), lambda i, j: (0, 0)),
            pl.BlockSpec((None, d_model, TN_IN), lambda i, j: (layer, 0, _first_view_tile(j))),
            pl.BlockSpec((None, d_model, TN_IN), lambda i, j: (layer, 0, jnp.maximum(j - J_SQ, 0))),
            pl.BlockSpec((None, d_model, LANES), lambda i, j: (layer, 0, 0)),
            pl.BlockSpec((1, TN_IN), lambda i, j: (0, jnp.maximum(j - J_GATE, 0))),
            pl.BlockSpec((1, LANES), lambda i, j: (0, 0)),
            pl.BlockSpec((CONV_K, WIDTH), lambda i, j: (0, 0)),
            pl.BlockSpec((1, TN_IN), lambda i, j: (0, jnp.clip(j - J_FQ, 0, 1))),
            pl.BlockSpec((CUM_BLK, CUM_BLK), lambda i, j: (0, 0)),
        ],
        out_specs=[
            pl.BlockSpec((seq, TN_IN), lambda i, j: (i, jnp.maximum(j - J_FQ, 0))),
            pl.BlockSpec((seq, WIDTH), lambda i, j: (i, 0)),
            pl.BlockSpec((seq, LANES), lambda i, j: (i, 0)),
        ],
        out_shape=[
            jax.ShapeDtypeStruct((tokens, n_out), BF16),
            jax.ShapeDtypeStruct((tokens, WIDTH), BF16),
            jax.ShapeDtypeStruct((tokens, LANES), F32),
        ],
        scratch_shapes=[
            pltpu.VMEM((seq, d_model), BF16),
            pltpu.VMEM((seq + HALO, WIDTH), F32),
        ],
        compiler_params=_cparams(("arbitrary", "arbitrary")),
        name="in_proj",
    )(x2d, g1, w_a, w_b, w_f, gate_bias, f_bias, conv_w, qk_gain, tri)


def _pair_masks(q_pair):
    lane = lax.broadcasted_iota(jnp.int32, q_pair.shape, 1)
    zero = jnp.zeros_like(q_pair)
    return (jnp.where(lane < HEAD_DIM, q_pair, zero), jnp.where(lane < HEAD_DIM, zero, q_pair))


def _merge_pair(o0, o1):
    return jnp.concatenate([o0[0:HEAD_DIM, :], o1[HEAD_DIM:2 * HEAD_DIM, :]], axis=0)


def _store_masked_queries(q_ref, qm_scr):
    for p in range(N_HEADS // 2):
        qm = _pair_masks(q_ref[:, p * LANES:(p + 1) * LANES])
        qm_scr[2 * p] = qm[0]
        qm_scr[2 * p + 1] = qm[1]


def _pair_cols(h):
    return slice((h // 2) * LANES, (h // 2 + 1) * LANES)


def _run_pipelined(qi, q0, scores, vector_stage, value_stage):
    vector_stage(scores(q0), q0, True)

    def body(i, _):
        k0 = pl.multiple_of((qi - 1 - i) * TK, TK)
        s_all = scores(k0)
        value_stage(pl.multiple_of(k0 + TK, TK))
        vector_stage(s_all, k0, False)
        return 0
    lax.fori_loop(0, qi, body, 0)
    value_stage(0)


def _fox_kernel(q_ref, k_ref, v_ref, c_ref, o_ref, qm_scr, acc_scr, p_scr, m_scr, l_scr, alpha_scr):
    qi = pl.program_id(1)
    q0 = pl.multiple_of(qi * TQ, TQ)
    krow = lax.broadcasted_iota(jnp.int32, (TK, TQ), 0)
    qcol = lax.broadcasted_iota(jnp.int32, (TK, TQ), 1)
    causal = krow <= qcol
    heads = range(N_HEADS)

    _store_masked_queries(q_ref, qm_scr)
    acc_scr[...] = jnp.zeros(acc_scr.shape, F32)
    m_scr[...] = jnp.full(m_scr.shape, NEG_INF, F32)
    l_scr[...] = jnp.zeros(l_scr.shape, F32)

    def scores(k0):
        return [_dot_nt(k_ref[pl.ds(k0, TK), _pair_cols(h)], qm_scr[h]) for h in heads]

    def vector_stage(s_all, k0, diag):
        for h in heads:
            s = s_all[h] - c_ref[pl.ds(k0, TK), h:h + 1]
            if diag:
                s = jnp.where(causal, s, NEG_INF)
            m = m_scr[h:h + 1, :]
            m_new = jnp.maximum(m, jnp.max(s, axis=0, keepdims=True))
            alpha = jnp.exp2(m - m_new)
            pr = jnp.exp2(s - m_new)
            m_scr[h:h + 1, :] = m_new
            l_scr[h:h + 1, :] = alpha * l_scr[h:h + 1, :] + jnp.sum(pr, axis=0, keepdims=True)
            alpha_scr[h:h + 1, :] = alpha
            p_scr[h] = pr.astype(BF16)

    def value_stage(k0):
        for h in heads:
            pv = _dot_tn(v_ref[pl.ds(k0, TK), _pair_cols(h)], p_scr[h])
            acc_scr[h] = alpha_scr[h:h + 1, :] * acc_scr[h] + pv

    _run_pipelined(qi, q0, scores, vector_stage, value_stage)
    for p in range(N_HEADS // 2):
        o_pair = _merge_pair(acc_scr[2 * p] / l_scr[2 * p:2 * p + 1, :],
                             acc_scr[2 * p + 1] / l_scr[2 * p + 1:2 * p + 2, :])
        o_ref[:, p * LANES:(p + 1) * LANES] = o_pair.T.astype(BF16)


def _sb_kernel(q_ref, k_ref, v_ref, tri_ref, o_ref, qm_scr, acc_scr, p_scr, r_scr):
    qi = pl.program_id(1)
    q0 = pl.multiple_of(qi * TQ, TQ)
    krow = lax.broadcasted_iota(jnp.int32, (TK, TQ), 0)
    qcol = lax.broadcasted_iota(jnp.int32, (TK, TQ), 1)
    strict = krow < qcol
    last_row = lax.broadcasted_iota(jnp.int32, (SUBLANES, TQ), 0) == SUBLANES - 1
    heads = range(N_HEADS)

    _store_masked_queries(q_ref, qm_scr)
    acc_scr[...] = jnp.zeros(acc_scr.shape, F32)
    r_scr[...] = jnp.zeros(r_scr.shape, F32)

    def scores(k0):
        return [_dot_nt(k_ref[pl.ds(k0, TK), _pair_cols(h)], qm_scr[h]) for h in heads]

    def vector_stage(z_all, k0, diag):
        la_all = []
        for h in heads:
            z = z_all[h]
            neg_abs = lax.bitcast_convert_type(
                lax.bitcast_convert_type(z, jnp.uint32) | jnp.uint32(0x80000000), F32)
            u = jnp.maximum(z, 0.0) + jnp.log(1.0 + jnp.exp2(neg_abs)) * LOG2E
            if diag:
                u = jnp.where(strict, u, 0.0)
            else:
                tail = u[TK - SUBLANES:, :] + jnp.where(last_row, r_scr[h:h + 1, :], 0.0)
                u = jnp.concatenate([u[:TK - SUBLANES, :], tail], axis=0)
            hi = u.astype(BF16)
            lo = (u - hi.astype(F32)).astype(BF16)
            tri = tri_ref[...]
            cs = _dot(tri, hi) + _dot(tri, lo)
            r_scr[h:h + 1, :] = cs[0:1, :]
            la_all.append(z - cs)
        for h in heads:
            a = jnp.exp2(la_all[h])
            if diag:
                a = jnp.where(strict, a, 0.0)
            p_scr[h] = a.astype(BF16)

    def value_stage(k0):
        for h in heads:
            acc_scr[h] = acc_scr[h] + _dot_tn(v_ref[pl.ds(k0, TK), _pair_cols(h)], p_scr[h])

    _run_pipelined(qi, q0, scores, vector_stage, value_stage)
    for p in range(N_HEADS // 2):
        o_pair = _merge_pair(acc_scr[2 * p], acc_scr[2 * p + 1])
        o_ref[:, p * LANES:(p + 1) * LANES] = o_pair.T.astype(BF16)


def _fox_attention(main, c, batch, seq):
    tokens = main.shape[0]
    nq = seq // TQ
    return pl.pallas_call(
        _fox_kernel,
        grid=(batch, nq),
        in_specs=[
            pl.BlockSpec((TQ, WIDTH), lambda b, qi: (b * nq + qi, 0)),
            pl.BlockSpec((seq, WIDTH), lambda b, qi: (b, 1)),
            pl.BlockSpec((seq, WIDTH), lambda b, qi: (b, 2)),
            pl.BlockSpec((seq, LANES), lambda b, qi: (b, 0)),
        ],
        out_specs=pl.BlockSpec((TQ, WIDTH), lambda b, qi: (b * nq + qi, 0)),
        out_shape=jax.ShapeDtypeStruct((tokens, WIDTH), BF16),
        scratch_shapes=[pltpu.VMEM((N_HEADS, TQ, LANES), BF16), pltpu.VMEM((N_HEADS, LANES, TQ), F32),
                        pltpu.VMEM((N_HEADS, TK, TQ), BF16), pltpu.VMEM((N_HEADS, TQ), F32),
                        pltpu.VMEM((N_HEADS, TQ), F32), pltpu.VMEM((N_HEADS, TQ), F32)],
        compiler_params=_cparams(("arbitrary", "arbitrary")),
        name="fox_attention",
    )(main, main, main, c)


def _sb_attention(main, tri, batch, seq):
    tokens = main.shape[0]
    nq = seq // TQ
    return pl.pallas_call(
        _sb_kernel,
        grid=(batch, nq),
        in_specs=[
            pl.BlockSpec((TQ, WIDTH), lambda b, qi: (b * nq + qi, 3)),
            pl.BlockSpec((seq, WIDTH), lambda b, qi: (b, 4)),
            pl.BlockSpec((seq, WIDTH), lambda b, qi: (b, 5)),
            pl.BlockSpec((TK, TK), lambda b, qi: (0, 0)),
        ],
        out_specs=pl.BlockSpec((TQ, WIDTH), lambda b, qi: (b * nq + qi, 0)),
        out_shape=jax.ShapeDtypeStruct((tokens, WIDTH), BF16),
        scratch_shapes=[pltpu.VMEM((N_HEADS, TQ, LANES), BF16), pltpu.VMEM((N_HEADS, LANES, TQ), F32),
                        pltpu.VMEM((N_HEADS, TK, TQ), BF16), pltpu.VMEM((N_HEADS, TQ), F32)],
        compiler_params=_cparams(("arbitrary", "arbitrary")),
        name="sb_attention",
    )(main, main, main, tri)


def _mix_kernel(yc_ref, af_ref, as_ref, gate_ref, x_ref, wc_ref, wf_ref, ws_ref, wo_ref, g2_ref,
                xo_ref, hn_ref):
    d_model = x_ref.shape[1]
    y = gate_ref[:, 0:d_model].astype(F32) * _dot(yc_ref[...], wc_ref[...])
    y = y + gate_ref[:, d_model:2 * d_model].astype(F32) * _dot(af_ref[...], wf_ref[...])
    y = y + gate_ref[:, 2 * d_model:3 * d_model].astype(F32) * _dot(as_ref[...], ws_ref[...])
    xn = x_ref[...] + _dot(y.astype(BF16), wo_ref[...])
    xo_ref[...] = xn
    ms = jnp.mean(xn * xn, axis=-1, keepdims=True)
    hn_ref[...] = (xn * lax.rsqrt(ms + NORM_EPS) * g2_ref[...]).astype(BF16)


def _mix(layer, yc, af, asb, main, x2d, wc, wf, ws, wo, g2):
    tokens, d_model = x2d.shape
    gate_blk = (main.shape[1] - 3 * d_model) // (3 * d_model)
    row = lambda i: (i, 0)
    const = lambda i: (0, 0)
    stacked = lambda i: (layer, 0, 0)
    return pl.pallas_call(
        _mix_kernel,
        grid=(tokens // TM_MIX,),
        in_specs=[
            pl.BlockSpec((TM_MIX, WIDTH), row),
            pl.BlockSpec((TM_MIX, WIDTH), row),
            pl.BlockSpec((TM_MIX, WIDTH), row),
            pl.BlockSpec((TM_MIX, 3 * d_model), lambda i: (i, gate_blk)),
            pl.BlockSpec((TM_MIX, d_model), row),
            pl.BlockSpec((None, WIDTH, d_model), stacked),
            pl.BlockSpec((None, WIDTH, d_model), stacked),
            pl.BlockSpec((None, WIDTH, d_model), stacked),
            pl.BlockSpec((None, d_model, d_model), stacked),
            pl.BlockSpec((1, d_model), const),
        ],
        out_specs=[pl.BlockSpec((TM_MIX, d_model), row), pl.BlockSpec((TM_MIX, d_model), row)],
        out_shape=[jax.ShapeDtypeStruct((tokens, d_model), F32),
                   jax.ShapeDtypeStruct((tokens, d_model), BF16)],
        compiler_params=_cparams(("arbitrary",)),
        name="mix",
    )(yc, af, asb, main, x2d, wc, wf, ws, wo, g2)


def _up_kernel(hn_ref, wg_ref, wv_ref, cw_ref, cb_ref, o_ref, ug_scr):
    seq = hn_ref.shape[0]
    ug_scr[0:HALO, :] = jnp.zeros((HALO, ug_scr.shape[1]), F32)
    for r in range(seq // MM_ROWS):
        rows = slice(r * MM_ROWS, (r + 1) * MM_ROWS)
        base = r * MM_ROWS + HALO
        ug_scr[base:base + MM_ROWS, :] = _dot(hn_ref[rows, :], wg_ref[...])
        uv = _dot(hn_ref[rows, :], wv_ref[...])
        u = cw_ref[2:3, :] * ug_scr[base:base + MM_ROWS, :] + cb_ref[...]
        u = u + cw_ref[1:2, :] * ug_scr[base - 1:base - 1 + MM_ROWS, :]
        u = u + cw_ref[0:1, :] * ug_scr[base - 2:base - 2 + MM_ROWS, :]
        o_ref[rows, :] = (u * _sigmoid(u) * uv).astype(BF16)


def _ffn_up(layer, hn2, w_up, conv_w, conv_b, batch, seq):
    tokens, d_model = hn2.shape
    d_ff = w_up.shape[2] // 2
    nj = d_ff // TN_UP
    return pl.pallas_call(
        _up_kernel,
        grid=(batch, nj),
        in_specs=[
            pl.BlockSpec((seq, d_model), lambda b, j: (b, 0)),
            pl.BlockSpec((None, d_model, TN_UP), lambda b, j: (layer, 0, j)),
            pl.BlockSpec((None, d_model, TN_UP), lambda b, j: (layer, 0, nj + j)),
            pl.BlockSpec((CONV_K, TN_UP), lambda b, j: (0, j)),
            pl.BlockSpec((1, TN_UP), lambda b, j: (0, j)),
        ],
        out_specs=pl.BlockSpec((seq, TN_UP), lambda b, j: (b, j)),
        out_shape=jax.ShapeDtypeStruct((tokens, d_ff), BF16),
        scratch_shapes=[pltpu.VMEM((seq + HALO, TN_UP), F32)],
        compiler_params=_cparams(("arbitrary", "arbitrary")),
        name="ffn_up",
    )(hn2, w_up, w_up, conv_w, conv_b)


def _down_kernel(h_ref, w_ref, x_ref, o_ref):
    o_ref[...] = x_ref[...] + _dot(h_ref[...], w_ref[...])


def _ffn_down(layer, hff, w_down, x2d):
    tokens, d_model = x2d.shape
    d_ff = hff.shape[1]
    return pl.pallas_call(
        _down_kernel,
        grid=(tokens // TM_DOWN,),
        in_specs=[
            pl.BlockSpec((TM_DOWN, d_ff), lambda i: (i, 0)),
            pl.BlockSpec((None, d_ff, d_model), lambda i: (layer, 0, 0)),
            pl.BlockSpec((TM_DOWN, d_model), lambda i: (i, 0)),
        ],
        out_specs=pl.BlockSpec((TM_DOWN, d_model), lambda i: (i, 0)),
        out_shape=jax.ShapeDtypeStruct((tokens, d_model), F32),
        compiler_params=_cparams(("arbitrary",)),
        name="ffn_down",
    )(hff, w_down, x2d)


def kernel(x, norm1_g, w_in, fox_f_bias, gate_bias, conv_w, fox_q_norm_g, fox_k_norm_g, w_proj_conv,
           w_proj_fox, w_proj_sb, w_out, norm2_g, w_up, ffn_conv_w, ffn_conv_b, w_down):
    batch, seq, d_model = x.shape
    depth = w_in.shape[0]
    scale = HEAD_DIM ** -0.5 * LOG2E
    assert seq % max(TQ, TK, CUM_BLK, ROW_CHUNK) == 0 and TQ == TK

    o_f = 6 * WIDTH
    o_sb = o_f + N_HEADS
    w_a = w_in[:, :, :o_f].astype(BF16)
    w_b = w_in[:, :, o_sb:].astype(BF16)
    w_f = jnp.pad(w_in[:, :, o_f:o_sb], ((0, 0), (0, 0), (0, LANES - N_HEADS))).astype(BF16)
    f_bias = jnp.pad(fox_f_bias, ((0, 0), (0, LANES - N_HEADS)))[:, None, :]
    qk_gain = jnp.concatenate([jnp.tile(fox_q_norm_g * scale, (1, N_HEADS)),
                               jnp.tile(fox_k_norm_g, (1, N_HEADS))], axis=-1)[:, None, :]
    wc, wf, ws, wo = (w.astype(BF16) for w in (w_proj_conv, w_proj_fox, w_proj_sb, w_out))
    w_up_b, w_down_b = w_up.astype(BF16), w_down.astype(BF16)

    idx = jnp.arange(CUM_BLK)
    tri_incl = (idx[None, :] <= idx[:, None]).astype(BF16)
    tri_suffix = (idx[None, :] >= idx[:, None]).astype(BF16)

    x2d = x.reshape(batch * seq, d_model)
    for l in range(depth):
        main, yc, c = _in_proj(l, x2d, norm1_g[l][None, :], w_a, w_b, w_f, gate_bias[l][None, :], f_bias[l],
                               conv_w[l], qk_gain[l], tri_incl, batch, seq, scale)
        af = _fox_attention(main, c, batch, seq)
        asb = _sb_attention(main, tri_suffix, batch, seq)
        x2d, hn2 = _mix(l, yc, af, asb, main, x2d, wc, wf, ws, wo, norm2_g[l][None, :])
        hff = _ffn_up(l, hn2, w_up_b, ffn_conv_w[l], ffn_conv_b[l][None, :], batch, seq)
        x2d = _ffn_down(l, hff, w_down_b, x2d)
    return x2d.reshape(batch, seq, d_model)
```

```python
import functools

import jax
import jax.numpy as jnp
from jax import lax
from jax.experimental import pallas as pl
from jax.experimental.pallas import tpu as pltpu

F32 = jnp.float32
BF16 = jnp.bfloat16

HEAD_DIM = 64
N_HEADS = 8
WIDTH = N_HEADS * HEAD_DIM
CONV_K = 3
NORM_EPS = 1e-6
NEG_INF = -1e30
LOG2E = 1.4426950408889634

LANES = 128
SUBLANES = 8
VMEM_LIMIT = 56 * 1024 * 1024

TN_IN = 512
ROW_CHUNK = 256
MM_ROWS = 512
CUM_BLK = 256
TQ = 256
TK = 256
TM_MIX = 512
TN_UP = 256
TM_DOWN = 512
HALO = SUBLANES


def _cparams(sem):
    return pltpu.CompilerParams(dimension_semantics=sem, vmem_limit_bytes=VMEM_LIMIT)


def _log_sigmoid(v):
    return jnp.minimum(v, 0.0) - jnp.log(1.0 + jnp.exp(-jnp.abs(v)))


def _split3(v):
    hi = v.astype(BF16)
    r1 = v - hi.astype(F32)
    mid = r1.astype(BF16)
    lo = (r1 - mid.astype(F32)).astype(BF16)
    return hi, mid, lo


def _dot(a, b):
    return jnp.dot(a, b, preferred_element_type=F32)


def _dot_nt(a, b):
    return lax.dot_general(a, b, (((1,), (1,)), ((), ())), preferred_element_type=F32)


def _dot_tn(a, b):
    return lax.dot_general(a, b, (((0,), (0,)), ((), ())), preferred_element_type=F32)


J_C, J_H, J_B, J_FQ, J_FK, J_FV, J_SQ, J_GATE = 0, 1, 2, 3, 4, 5, 6, 9


def _first_view_tile(j):
    return jnp.where(j < J_FQ, (j + 1) % 3, jnp.minimum(j, J_FV))


def _sigmoid(v):
    return 0.5 * jnp.tanh(0.5 * v) + 0.5


def _in_kernel(x_ref, g1_ref, wa_ref, wb_ref, wf_ref, bias_ref, fb_ref, cw_ref, qkg_ref, tri_ref,
               main_ref, yc_ref, c_ref, hn_scr, ch_scr, *, sb_scale):
    j = pl.program_id(1)
    seq = x_ref.shape[0]

    def tile(epilogue, w_ref):
        for r in range(seq // MM_ROWS):
            rows = slice(r * MM_ROWS, (r + 1) * MM_ROWS)
            epilogue(r, rows, _dot(hn_scr[rows, :], w_ref[...]))

    def store_c(r, rows, acc):
        ch_scr[HALO + r * MM_ROWS:HALO + (r + 1) * MM_ROWS, :] = acc

    @pl.when(j == J_C)
    def _():
        def norm_body(r, _):
            rows = pl.ds(pl.multiple_of(r * ROW_CHUNK, ROW_CHUNK), ROW_CHUNK)
            x = x_ref[rows, :]
            ms = jnp.mean(x * x, axis=-1, keepdims=True)
            hn_scr[rows, :] = (x * lax.rsqrt(ms + NORM_EPS) * g1_ref[...]).astype(BF16)
            return 0
        lax.fori_loop(0, seq // ROW_CHUNK, norm_body, 0)
        ch_scr[0:HALO, :] = jnp.zeros((HALO, ch_scr.shape[1]), F32)
        tile(store_c, wa_ref)

        blocks = [slice(blk * CUM_BLK, (blk + 1) * CUM_BLK) for blk in range(seq // CUM_BLK)]
        f_all = [_dot(hn_scr[rows, :], wf_ref[...]) for rows in blocks]
        parts = [_split3(_log_sigmoid(f + fb_ref[...]) * LOG2E) for f in f_all]
        carry = jnp.zeros((1, LANES), F32)
        for rows, (hi, mid, lo) in zip(blocks, parts):
            tri = tri_ref[...]
            cs = _dot(tri, hi) + _dot(tri, mid) + _dot(tri, lo) + carry
            c_ref[rows, :] = cs
            carry = cs[CUM_BLK - 1:CUM_BLK, :]

    @pl.when(j == J_H)
    def _():
        def fn(r, rows, acc):
            dst = slice(HALO + r * MM_ROWS, HALO + (r + 1) * MM_ROWS)
            ch_scr[dst, :] = ch_scr[dst, :] * acc
        tile(fn, wa_ref)

    @pl.when(j == J_B)
    def _():
        def fn(r, rows, acc):
            base = r * MM_ROWS + HALO
            u = cw_ref[2:3, :] * ch_scr[base:base + MM_ROWS, :]
            u = u + cw_ref[1:2, :] * ch_scr[base - 1:base - 1 + MM_ROWS, :]
            u = u + cw_ref[0:1, :] * ch_scr[base - 2:base - 2 + MM_ROWS, :]
            yc_ref[rows, :] = (acc * u).astype(BF16)
        tile(fn, wa_ref)

    @pl.when((j == J_FQ) | (j == J_FK))
    def _():
        lane = lax.broadcasted_iota(jnp.int32, (MM_ROWS, LANES), 1)
        low = lane < HEAD_DIM

        def fn(r, rows, acc):
            for cidx in range(TN_IN // LANES):
                cols = slice(cidx * LANES, (cidx + 1) * LANES)
                a = acc[:, cols]
                a2 = a * a
                s_lo = jnp.sum(jnp.where(low, a2, 0.0), axis=-1, keepdims=True)
                s_hi = jnp.sum(jnp.where(low, 0.0, a2), axis=-1, keepdims=True)
                ms = jnp.where(low, s_lo, s_hi) * (1.0 / HEAD_DIM)
                main_ref[rows, cols] = (a * lax.rsqrt(ms + NORM_EPS) * qkg_ref[:, cols]).astype(BF16)
        tile(fn, wa_ref)

    def store_raw(r, rows, acc):
        main_ref[rows, :] = acc.astype(BF16)

    @pl.when(j == J_FV)
    def _():
        tile(store_raw, wa_ref)

    @pl.when(j == J_SQ)
    def _():
        def fn(r, rows, acc):
            main_ref[rows, :] = (acc * sb_scale).astype(BF16)
        tile(fn, wb_ref)

    @pl.when((j > J_SQ) & (j < J_GATE))
    def _():
        tile(store_raw, wb_ref)

    @pl.when(j >= J_GATE)
    def _():
        def fn(r, rows, acc):
            main_ref[rows, :] = _sigmoid(acc + bias_ref[...]).astype(BF16)
        tile(fn, wb_ref)


def _in_proj(layer, x2d, g1, w_a, w_b, w_f, gate_bias, f_bias, conv_w, qk_gain, tri, batch, seq, sb_scale):
    d_model = x2d.shape[1]
    assert w_a.shape[2] == (J_FV + 1) * TN_IN and gate_bias.shape[1] % TN_IN == 0
    nj = (w_a.shape[2] + w_b.shape[2]) // TN_IN
    n_out = (nj - J_FQ) * TN_IN
    tokens = x2d.shape[0]
    return pl.pallas_call(
        functools.partial(_in_kernel, sb_scale=sb_scale),
        grid=(batch, nj),
        in_specs=[
            pl.BlockSpec((seq, d_model), lambda i, j: (i, 0)),
            pl.BlockSpec((1, d_model), lambda i, j: (0, 0)),
            pl.BlockSpec((None, d_model, TN_IN), lambda i, j: (layer, 0, _first_view_tile(j))),
            pl.BlockSpec((None, d_model, TN_IN), lambda i, j: (layer, 0, jnp.maximum(j - J_SQ, 0))),
            pl.BlockSpec((None, d_model, LANES), lambda i, j: (layer, 0, 0)),
            pl.BlockSpec((1, TN_IN), lambda i, j: (0, jnp.maximum(j - J_GATE, 0))),
            pl.BlockSpec((1, LANES), lambda i, j: (0, 0)),
            pl.BlockSpec((CONV_K, WIDTH), lambda i, j: (0, 0)),
            pl.BlockSpec((1, TN_IN), lambda i, j: (0, jnp.clip(j - J_FQ, 0, 1))),
            pl.BlockSpec((CUM_BLK, CUM_BLK), lambda i, j: (0, 0)),
        ],
        out_specs=[
            pl.BlockSpec((seq, TN_IN), lambda i, j: (i, jnp.maximum(j - J_FQ, 0))),
            pl.BlockSpec((seq, WIDTH), lambda i, j: (i, 0)),
            pl.BlockSpec((seq, LANES), lambda i, j: (i, 0)),
        ],
        out_shape=[
            jax.ShapeDtypeStruct((tokens, n_out), BF16),
            jax.ShapeDtypeStruct((tokens, WIDTH), BF16),
            jax.ShapeDtypeStruct((tokens, LANES), F32),
        ],
        scratch_shapes=[
            pltpu.VMEM((seq, d_model), BF16),
            pltpu.VMEM((seq + HALO, WIDTH), F32),
        ],
        compiler_params=_cparams(("arbitrary", "arbitrary")),
        name="in_proj",
    )(x2d, g1, w_a, w_b, w_f, gate_bias, f_bias, conv_w, qk_gain, tri)


def _pair_masks(q_pair):
    lane = lax.broadcasted_iota(jnp.int32, q_pair.shape, 1)
    zero = jnp.zeros_like(q_pair)
    return (jnp.where(lane < HEAD_DIM, q_pair, zero), jnp.where(lane < HEAD_DIM, zero, q_pair))


def _merge_pair(o0, o1):
    return jnp.concatenate([o0[0:HEAD_DIM, :], o1[HEAD_DIM:2 * HEAD_DIM, :]], axis=0)


def _store_masked_queries(q_ref, qm_scr):
    for p in range(N_HEADS // 2):
        qm = _pair_masks(q_ref[:, p * LANES:(p + 1) * LANES])
        qm_scr[2 * p] = qm[0]
        qm_scr[2 * p + 1] = qm[1]


def _pair_cols(h):
    return slice((h // 2) * LANES, (h // 2 + 1) * LANES)


def _run_pipelined(qi, q0, scores, vector_stage, value_stage):
    vector_stage(scores(q0), q0, True)

    def pair_body(i, _):
        k_a = pl.multiple_of((qi - 1 - 2 * i) * TK, TK)
        k_b = pl.multiple_of(k_a - TK, TK)
        s_a = scores(k_a)
        value_stage(pl.multiple_of(k_a + TK, TK))
        s_b = scores(k_b)
        vector_stage(s_a, k_a, False)
        value_stage(k_a)
        vector_stage(s_b, k_b, False)
        return 0
    lax.fori_loop(0, qi // 2, pair_body, 0)

    @pl.when(qi % 2 == 1)
    def _():
        s_all = scores(0)
        value_stage(TK)
        vector_stage(s_all, 0, False)
    value_stage(0)


def _fox_kernel(q_ref, k_ref, v_ref, c_ref, o_ref, qm_scr, acc_scr, p_scr, m_scr, l_scr, alpha_scr):
    qi = pl.program_id(1)
    q0 = pl.multiple_of(qi * TQ, TQ)
    krow = lax.broadcasted_iota(jnp.int32, (TK, TQ), 0)
    qcol = lax.broadcasted_iota(jnp.int32, (TK, TQ), 1)
    causal = krow <= qcol
    heads = range(N_HEADS)

    _store_masked_queries(q_ref, qm_scr)
    acc_scr[...] = jnp.zeros(acc_scr.shape, F32)
    m_scr[...] = jnp.full(m_scr.shape, NEG_INF, F32)
    l_scr[...] = jnp.zeros(l_scr.shape, F32)

    def scores(k0):
        return [_dot_nt(k_ref[pl.ds(k0, TK), _pair_cols(h)], qm_scr[h]) for h in heads]

    def vector_stage(s_all, k0, diag):
        for h in heads:
            s = s_all[h] - c_ref[pl.ds(k0, TK), h:h + 1]
            if diag:
                s = jnp.where(causal, s, NEG_INF)
            m = m_scr[h:h + 1, :]
            m_new = jnp.maximum(m, jnp.max(s, axis=0, keepdims=True))
            alpha = jnp.exp2(m - m_new)
            pr = jnp.exp2(s - m_new)
            m_scr[h:h + 1, :] = m_new
            l_scr[h:h + 1, :] = alpha * l_scr[h:h + 1, :] + jnp.sum(pr, axis=0, keepdims=True)
            alpha_scr[h:h + 1, :] = alpha
            p_scr[h] = pr.astype(BF16)

    def value_stage(k0):
        for h in heads:
            pv = _dot_tn(v_ref[pl.ds(k0, TK), _pair_cols(h)], p_scr[h])
            acc_scr[h] = alpha_scr[h:h + 1, :] * acc_scr[h] + pv

    _run_pipelined(qi, q0, scores, vector_stage, value_stage)
    for p in range(N_HEADS // 2):
        o_pair = _merge_pair(acc_scr[2 * p] / l_scr[2 * p:2 * p + 1, :],
                             acc_scr[2 * p + 1] / l_scr[2 * p + 1:2 * p + 2, :])
        o_ref[:, p * LANES:(p + 1) * LANES] = o_pair.T.astype(BF16)


def _sb_kernel(q_ref, k_ref, v_ref, tri_ref, o_ref, qm_scr, acc_scr, p_scr, r_scr):
    qi = pl.program_id(1)
    q0 = pl.multiple_of(qi * TQ, TQ)
    krow = lax.broadcasted_iota(jnp.int32, (TK, TQ), 0)
    qcol = lax.broadcasted_iota(jnp.int32, (TK, TQ), 1)
    strict = krow < qcol
    heads = range(N_HEADS)

    _store_masked_queries(q_ref, qm_scr)
    acc_scr[...] = jnp.zeros(acc_scr.shape, F32)
    r_scr[...] = jnp.zeros(r_scr.shape, F32)

    def scores(k0):
        return [_dot_nt(k_ref[pl.ds(k0, TK), _pair_cols(h)], qm_scr[h]) for h in heads]

    def vector_stage(z_all, k0, diag):
        la_all = []
        for h in heads:
            z = z_all[h]
            neg_abs = lax.bitcast_convert_type(
                lax.bitcast_convert_type(z, jnp.uint32) | jnp.uint32(0x80000000), F32)
            u = jnp.maximum(z, 0.0) + jnp.log(1.0 + jnp.exp2(neg_abs)) * LOG2E
            if diag:
                u = jnp.where(strict, u, 0.0)
            cs = _dot(tri_ref[...], u.astype(BF16))
            r = r_scr[h:h + 1, :]
            r_scr[h:h + 1, :] = r + cs[0:1, :]
            la_all.append((z - r) - cs)
        for h in heads:
            a = jnp.exp2(la_all[h])
            if diag:
                a = jnp.where(strict, a, 0.0)
            p_scr[h] = a.astype(BF16)

    def value_stage(k0):
        for h in heads:
            acc_scr[h] = acc_scr[h] + _dot_tn(v_ref[pl.ds(k0, TK), _pair_cols(h)], p_scr[h])

    _run_pipelined(qi, q0, scores, vector_stage, value_stage)
    for p in range(N_HEADS // 2):
        o_pair = _merge_pair(acc_scr[2 * p], acc_scr[2 * p + 1])
        o_ref[:, p * LANES:(p + 1) * LANES] = o_pair.T.astype(BF16)


def _fox_attention(main, c, batch, seq):
    tokens = main.shape[0]
    nq = seq // TQ
    return pl.pallas_call(
        _fox_kernel,
        grid=(batch, nq),
        in_specs=[
            pl.BlockSpec((TQ, WIDTH), lambda b, qi: (b * nq + qi, 0)),
            pl.BlockSpec((seq, WIDTH), lambda b, qi: (b, 1)),
            pl.BlockSpec((seq, WIDTH), lambda b, qi: (b, 2)),
            pl.BlockSpec((seq, LANES), lambda b, qi: (b, 0)),
        ],
        out_specs=pl.BlockSpec((TQ, WIDTH), lambda b, qi: (b * nq + qi, 0)),
        out_shape=jax.ShapeDtypeStruct((tokens, WIDTH), BF16),
        scratch_shapes=[pltpu.VMEM((N_HEADS, TQ, LANES), BF16), pltpu.VMEM((N_HEADS, LANES, TQ), F32),
                        pltpu.VMEM((N_HEADS, TK, TQ), BF16), pltpu.VMEM((N_HEADS, TQ), F32),
                        pltpu.VMEM((N_HEADS, TQ), F32), pltpu.VMEM((N_HEADS, TQ), F32)],
        compiler_params=_cparams(("arbitrary", "arbitrary")),
        name="fox_attention",
    )(main, main, main, c)


def _sb_attention(main, tri, batch, seq):
    tokens = main.shape[0]
    nq = seq // TQ
    return pl.pallas_call(
        _sb_kernel,
        grid=(batch, nq),
        in_specs=[
            pl.BlockSpec((TQ, WIDTH), lambda b, qi: (b * nq + qi, 3)),
            pl.BlockSpec((seq, WIDTH), lambda b, qi: (b, 4)),
            pl.BlockSpec((seq, WIDTH), lambda b, qi: (b, 5)),
            pl.BlockSpec((TK, TK), lambda b, qi: (0, 0)),
        ],
        out_specs=pl.BlockSpec((TQ, WIDTH), lambda b, qi: (b * nq + qi, 0)),
        out_shape=jax.ShapeDtypeStruct((tokens, WIDTH), BF16),
        scratch_shapes=[pltpu.VMEM((N_HEADS, TQ, LANES), BF16), pltpu.VMEM((N_HEADS, LANES, TQ), F32),
                        pltpu.VMEM((N_HEADS, TK, TQ), BF16), pltpu.VMEM((N_HEADS, TQ), F32)],
        compiler_params=_cparams(("arbitrary", "arbitrary")),
        name="sb_attention",
    )(main, main, main, tri)


def _mix_kernel(yc_ref, af_ref, as_ref, gate_ref, x_ref, wc_ref, wf_ref, ws_ref, wo_ref, g2_ref,
                xo_ref, hn_ref):
    d_model = x_ref.shape[1]
    y = gate_ref[:, 0:d_model].astype(F32) * _dot(yc_ref[...], wc_ref[...])
    y = y + gate_ref[:, d_model:2 * d_model].astype(F32) * _dot(af_ref[...], wf_ref[...])
    y = y + gate_ref[:, 2 * d_model:3 * d_model].astype(F32) * _dot(as_ref[...], ws_ref[...])
    xn = x_ref[...] + _dot(y.astype(BF16), wo_ref[...])
    xo_ref[...] = xn
    ms = jnp.mean(xn * xn, axis=-1, keepdims=True)
    hn_ref[...] = (xn * lax.rsqrt(ms + NORM_EPS) * g2_ref[...]).astype(BF16)


def _mix(layer, yc, af, asb, main, x2d, wc, wf, ws, wo, g2):
    tokens, d_model = x2d.shape
    gate_blk = (main.shape[1] - 3 * d_model) // (3 * d_model)
    row = lambda i: (i, 0)
    const = lambda i: (0, 0)
    stacked = lambda i: (layer, 0, 0)
    return pl.pallas_call(
        _mix_kernel,
        grid=(tokens // TM_MIX,),
        in_specs=[
            pl.BlockSpec((TM_MIX, WIDTH), row),
            pl.BlockSpec((TM_MIX, WIDTH), row),
            pl.BlockSpec((TM_MIX, WIDTH), row),
            pl.BlockSpec((TM_MIX, 3 * d_model), lambda i: (i, gate_blk)),
            pl.BlockSpec((TM_MIX, d_model), row),
            pl.BlockSpec((None, WIDTH, d_model), stacked),
            pl.BlockSpec((None, WIDTH, d_model), stacked),
            pl.BlockSpec((None, WIDTH, d_model), stacked),
            pl.BlockSpec((None, d_model, d_model), stacked),
            pl.BlockSpec((1, d_model), const),
        ],
        out_specs=[pl.BlockSpec((TM_MIX, d_model), row), pl.BlockSpec((TM_MIX, d_model), row)],
        out_shape=[jax.ShapeDtypeStruct((tokens, d_model), F32),
                   jax.ShapeDtypeStruct((tokens, d_model), BF16)],
        compiler_params=_cparams(("arbitrary",)),
        name="mix",
    )(yc, af, asb, main, x2d, wc, wf, ws, wo, g2)


def _up_kernel(hn_ref, wg_ref, wv_ref, cw_ref, cb_ref, o_ref, ug_scr):
    seq = hn_ref.shape[0]
    ug_scr[0:HALO, :] = jnp.zeros((HALO, ug_scr.shape[1]), F32)
    for r in range(seq // MM_ROWS):
        rows = slice(r * MM_ROWS, (r + 1) * MM_ROWS)
        base = r * MM_ROWS + HALO
        ug_scr[base:base + MM_ROWS, :] = _dot(hn_ref[rows, :], wg_ref[...])
        uv = _dot(hn_ref[rows, :], wv_ref[...])
        u = cw_ref[2:3, :] * ug_scr[base:base + MM_ROWS, :] + cb_ref[...]
        u = u + cw_ref[1:2, :] * ug_scr[base - 1:base - 1 + MM_ROWS, :]
        u = u + cw_ref[0:1, :] * ug_scr[base - 2:base - 2 + MM_ROWS, :]
        o_ref[rows, :] = (u * _sigmoid(u) * uv).astype(BF16)


def _ffn_up(layer, hn2, w_up, conv_w, conv_b, batch, seq):
    tokens, d_model = hn2.shape
    d_ff = w_up.shape[2] // 2
    nj = d_ff // TN_UP
    return pl.pallas_call(
        _up_kernel,
        grid=(batch, nj),
        in_specs=[
            pl.BlockSpec((seq, d_model), lambda b, j: (b, 0)),
            pl.BlockSpec((None, d_model, TN_UP), lambda b, j: (layer, 0, j)),
            pl.BlockSpec((None, d_model, TN_UP), lambda b, j: (layer, 0, nj + j)),
            pl.BlockSpec((CONV_K, TN_UP), lambda b, j: (0, j)),
            pl.BlockSpec((1, TN_UP), lambda b, j: (0, j)),
        ],
        out_specs=pl.BlockSpec((seq, TN_UP), lambda b, j: (b, j)),
        out_shape=jax.ShapeDtypeStruct((tokens, d_ff), BF16),
        scratch_shapes=[pltpu.VMEM((seq + HALO, TN_UP), F32)],
        compiler_params=_cparams(("arbitrary", "arbitrary")),
        name="ffn_up",
    )(hn2, w_up, w_up, conv_w, conv_b)


def _down_kernel(h_ref, w_ref, x_ref, o_ref):
    o_ref[...] = x_ref[...] + _dot(h_ref[...], w_ref[...])


def _ffn_down(layer, hff, w_down, x2d):
    tokens, d_model = x2d.shape
    d_ff = hff.shape[1]
    return pl.pallas_call(
        _down_kernel,
        grid=(tokens // TM_DOWN,),
        in_specs=[
            pl.BlockSpec((TM_DOWN, d_ff), lambda i: (i, 0)),
            pl.BlockSpec((None, d_ff, d_model), lambda i: (layer, 0, 0)),
            pl.BlockSpec((TM_DOWN, d_model), lambda i: (i, 0)),
        ],
        out_specs=pl.BlockSpec((TM_DOWN, d_model), lambda i: (i, 0)),
        out_shape=jax.ShapeDtypeStruct((tokens, d_model), F32),
        compiler_params=_cparams(("arbitrary",)),
        name="ffn_down",
    )(hff, w_down, x2d)


def kernel(x, norm1_g, w_in, fox_f_bias, gate_bias, conv_w, fox_q_norm_g, fox_k_norm_g, w_proj_conv,
           w_proj_fox, w_proj_sb, w_out, norm2_g, w_up, ffn_conv_w, ffn_conv_b, w_down):
    batch, seq, d_model = x.shape
    depth = w_in.shape[0]
    scale = HEAD_DIM ** -0.5 * LOG2E
    assert seq % max(TQ, TK, CUM_BLK, ROW_CHUNK) == 0 and TQ == TK

    o_f = 6 * WIDTH
    o_sb = o_f + N_HEADS
    w_a = w_in[:, :, :o_f].astype(BF16)
    w_b = w_in[:, :, o_sb:].astype(BF16)
    w_f = jnp.pad(w_in[:, :, o_f:o_sb], ((0, 0), (0, 0), (0, LANES - N_HEADS))).astype(BF16)
    f_bias = jnp.pad(fox_f_bias, ((0, 0), (0, LANES - N_HEADS)))[:, None, :]
    qk_gain = jnp.concatenate([jnp.tile(fox_q_norm_g * scale, (1, N_HEADS)),
                               jnp.tile(fox_k_norm_g, (1, N_HEADS))], axis=-1)[:, None, :]
    wc, wf, ws, wo = (w.astype(BF16) for w in (w_proj_conv, w_proj_fox, w_proj_sb, w_out))
    w_up_b, w_down_b = w_up.astype(BF16), w_down.astype(BF16)

    idx = jnp.arange(CUM_BLK)
    tri_incl = (idx[None, :] <= idx[:, None]).astype(BF16)
    tri_suffix = (idx[None, :] >= idx[:, None]).astype(BF16)

    x2d = x.reshape(batch * seq, d_model)
    for l in range(depth):
        main, yc, c = _in_proj(l, x2d, norm1_g[l][None, :], w_a, w_b, w_f, gate_bias[l][None, :], f_bias[l],
                               conv_w[l], qk_gain[l], tri_incl, batch, seq, scale)
        af = _fox_attention(main, c, batch, seq)
        asb = _sb_attention(main, tri_suffix, batch, seq)
        x2d, hn2 = _mix(l, yc, af, asb, main, x2d, wc, wf, ws, wo, norm2_g[l][None, :])
        hff = _ffn_up(l, hn2, w_up_b, ffn_conv_w[l], ffn_conv_b[l][None, :], batch, seq)
        x2d = _ffn_down(l, hff, w_down_b, x2d)
    return x2d.reshape(batch, seq, d_model)
```

```python
import functools

import jax
import jax.numpy as jnp
from jax import lax
from jax.experimental import pallas as pl
from jax.experimental.pallas import tpu as pltpu

F32 = jnp.float32
BF16 = jnp.bfloat16

HEAD_DIM = 64
N_HEADS = 8
WIDTH = N_HEADS * HEAD_DIM
CONV_K = 3
NORM_EPS = 1e-6
NEG_INF = -1e30
LOG2E = 1.4426950408889634

LANES = 128
SUBLANES = 8
VMEM_LIMIT = 56 * 1024 * 1024

TN_IN = 512
ROW_CHUNK = 256
MM_ROWS = 512
CUM_BLK = 256
TQ = 256
TK = 256
TM_MIX = 512
TN_UP = 256
HALO = SUBLANES


def _cparams(sem):
    return pltpu.CompilerParams(dimension_semantics=sem, vmem_limit_bytes=VMEM_LIMIT)


def _log_sigmoid(v):
    return jnp.minimum(v, 0.0) - jnp.log(1.0 + jnp.exp(-jnp.abs(v)))


def _split3(v):
    hi = v.astype(BF16)
    r1 = v - hi.astype(F32)
    mid = r1.astype(BF16)
    lo = (r1 - mid.astype(F32)).astype(BF16)
    return hi, mid, lo


def _dot(a, b):
    return jnp.dot(a, b, preferred_element_type=F32)


def _dot_nt(a, b):
    return lax.dot_general(a, b, (((1,), (1,)), ((), ())), preferred_element_type=F32)


def _dot_tn(a, b):
    return lax.dot_general(a, b, (((0,), (0,)), ((), ())), preferred_element_type=F32)


J_C, J_H, J_B, J_FQ, J_FK, J_FV, J_SQ, J_GATE = 0, 1, 2, 3, 4, 5, 6, 9


def _first_view_tile(j):
    return jnp.where(j < J_FQ, (j + 1) % 3, jnp.minimum(j, J_FV))


def _sigmoid(v):
    return 0.5 * jnp.tanh(0.5 * v) + 0.5


def _in_kernel(x_ref, g1_ref, wa_ref, wb_ref, wf_ref, bias_ref, fb_ref, cw_ref, qkg_ref, tri_ref,
               main_ref, yc_ref, c_ref, hn_scr, ch_scr, *, sb_scale):
    j = pl.program_id(1)
    seq = x_ref.shape[0]

    def tile(epilogue, w_ref):
        for r in range(seq // MM_ROWS):
            rows = slice(r * MM_ROWS, (r + 1) * MM_ROWS)
            epilogue(r, rows, _dot(hn_scr[rows, :], w_ref[...]))

    def store_c(r, rows, acc):
        ch_scr[HALO + r * MM_ROWS:HALO + (r + 1) * MM_ROWS, :] = acc

    @pl.when(j == J_C)
    def _():
        def norm_body(r, _):
            rows = pl.ds(pl.multiple_of(r * ROW_CHUNK, ROW_CHUNK), ROW_CHUNK)
            x = x_ref[rows, :]
            ms = jnp.mean(x * x, axis=-1, keepdims=True)
            hn_scr[rows, :] = (x * lax.rsqrt(ms + NORM_EPS) * g1_ref[...]).astype(BF16)
            return 0
        lax.fori_loop(0, seq // ROW_CHUNK, norm_body, 0)
        ch_scr[0:HALO, :] = jnp.zeros((HALO, ch_scr.shape[1]), F32)
        tile(store_c, wa_ref)

        blocks = [slice(blk * CUM_BLK, (blk + 1) * CUM_BLK) for blk in range(seq // CUM_BLK)]
        f_all = [_dot(hn_scr[rows, :], wf_ref[...]) for rows in blocks]
        parts = [_split3(_log_sigmoid(f + fb_ref[...]) * LOG2E) for f in f_all]
        carry = jnp.zeros((1, LANES), F32)
        for rows, (hi, mid, lo) in zip(blocks, parts):
            tri = tri_ref[...]
            cs = _dot(tri, hi) + _dot(tri, mid) + _dot(tri, lo) + carry
            c_ref[rows, :] = cs
            carry = cs[CUM_BLK - 1:CUM_BLK, :]

    @pl.when(j == J_H)
    def _():
        def fn(r, rows, acc):
            dst = slice(HALO + r * MM_ROWS, HALO + (r + 1) * MM_ROWS)
            ch_scr[dst, :] = ch_scr[dst, :] * acc
        tile(fn, wa_ref)

    @pl.when(j == J_B)
    def _():
        def fn(r, rows, acc):
            base = r * MM_ROWS + HALO
            u = cw_ref[2:3, :] * ch_scr[base:base + MM_ROWS, :]
            u = u + cw_ref[1:2, :] * ch_scr[base - 1:base - 1 + MM_ROWS, :]
            u = u + cw_ref[0:1, :] * ch_scr[base - 2:base - 2 + MM_ROWS, :]
            yc_ref[rows, :] = (acc * u).astype(BF16)
        tile(fn, wa_ref)

    @pl.when((j == J_FQ) | (j == J_FK))
    def _():
        lane = lax.broadcasted_iota(jnp.int32, (MM_ROWS, LANES), 1)
        low = lane < HEAD_DIM

        def fn(r, rows, acc):
            for cidx in range(TN_IN // LANES):
                cols = slice(cidx * LANES, (cidx + 1) * LANES)
                a = acc[:, cols]
                a2 = a * a
                s_lo = jnp.sum(jnp.where(low, a2, 0.0), axis=-1, keepdims=True)
                s_hi = jnp.sum(jnp.where(low, 0.0, a2), axis=-1, keepdims=True)
                ms = jnp.where(low, s_lo, s_hi) * (1.0 / HEAD_DIM)
                main_ref[rows, cols] = (a * lax.rsqrt(ms + NORM_EPS) * qkg_ref[:, cols]).astype(BF16)
        tile(fn, wa_ref)

    def store_raw(r, rows, acc):
        main_ref[rows, :] = acc.astype(BF16)

    @pl.when(j == J_FV)
    def _():
        tile(store_raw, wa_ref)

    @pl.when(j == J_SQ)
    def _():
        def fn(r, rows, acc):
            main_ref[rows, :] = (acc * sb_scale).astype(BF16)
        tile(fn, wb_ref)

    @pl.when((j > J_SQ) & (j < J_GATE))
    def _():
        tile(store_raw, wb_ref)

    @pl.when(j >= J_GATE)
    def _():
        def fn(r, rows, acc):
            main_ref[rows, :] = _sigmoid(acc + bias_ref[...]).astype(BF16)
        tile(fn, wb_ref)


def _in_proj(layer, x2d, g1, w_a, w_b, w_f, gate_bias, f_bias, conv_w, qk_gain, tri, batch, seq, sb_scale):
    d_model = x2d.shape[1]
    assert w_a.shape[2] == (J_FV + 1) * TN_IN and gate_bias.shape[1] % TN_IN == 0
    nj = (w_a.shape[2] + w_b.shape[2]) // TN_IN
    n_out = (nj - J_FQ) * TN_IN
    tokens = x2d.shape[0]
    return pl.pallas_call(
        functools.partial(_in_kernel, sb_scale=sb_scale),
        grid=(batch, nj),
        in_specs=[
            pl.BlockSpec((seq, d_model), lambda i, j: (i, 0)),
            pl.BlockSpec((1, d_model), lambda i, j: (0, 0)),
            pl.BlockSpec((None, d_model, TN_IN), lambda i, j: (layer, 0, _first_view_tile(j))),
            pl.BlockSpec((None, d_model, TN_IN), lambda i, j: (layer, 0, jnp.maximum(j - J_SQ, 0))),
            pl.BlockSpec((None, d_model, LANES), lambda i, j: (layer, 0, 0)),
            pl.BlockSpec((1, TN_IN), lambda i, j: (0, jnp.maximum(j - J_GATE, 0))),
            pl.BlockSpec((1, LANES), lambda i, j: (0, 0)),
            pl.BlockSpec((CONV_K, WIDTH), lambda i, j: (0, 0)),
            pl.BlockSpec((1, TN_IN), lambda i, j: (0, jnp.clip(j - J_FQ, 0, 1))),
            pl.BlockSpec((CUM_BLK, CUM_BLK), lambda i, j: (0, 0)),
        ],
        out_specs=[
            pl.BlockSpec((seq, TN_IN), lambda i, j: (i, jnp.maximum(j - J_FQ, 0))),
            pl.BlockSpec((seq, WIDTH), lambda i, j: (i, 0)),
            pl.BlockSpec((seq, LANES), lambda i, j: (i, 0)),
        ],
        out_shape=[
            jax.ShapeDtypeStruct((tokens, n_out), BF16),
            jax.ShapeDtypeStruct((tokens, WIDTH), BF16),
            jax.ShapeDtypeStruct((tokens, LANES), F32),
        ],
        scratch_shapes=[
            pltpu.VMEM((seq, d_model), BF16),
            pltpu.VMEM((seq + HALO, WIDTH), F32),
        ],
        compiler_params=_cparams(("arbitrary", "arbitrary")),
        name="in_proj",
    )(x2d, g1, w_a, w_b, w_f, gate_bias, f_bias, conv_w, qk_gain, tri)


def _pair_masks(q_pair):
    lane = lax.broadcasted_iota(jnp.int32, q_pair.shape, 1)
    zero = jnp.zeros_like(q_pair)
    return (jnp.where(lane < HEAD_DIM, q_pair, zero), jnp.where(lane < HEAD_DIM, zero, q_pair))


def _merge_pair(o0, o1):
    return jnp.concatenate([o0[0:HEAD_DIM, :], o1[HEAD_DIM:2 * HEAD_DIM, :]], axis=0)


def _store_masked_queries(q_ref, qm_scr):
    for p in range(N_HEADS // 2):
        qm = _pair_masks(q_ref[:, p * LANES:(p + 1) * LANES])
        qm_scr[2 * p] = qm[0]
        qm_scr[2 * p + 1] = qm[1]


def _pair_cols(h):
    return slice((h // 2) * LANES, (h // 2 + 1) * LANES)


def _run_pipelined(qi, q0, scores, vector_stage, value_stage):
    vector_stage(scores(q0), q0, True)

    def pair_body(i, _):
        k_a = pl.multiple_of((qi - 1 - 2 * i) * TK, TK)
        k_b = pl.multiple_of(k_a - TK, TK)
        s_a = scores(k_a)
        value_stage(pl.multiple_of(k_a + TK, TK))
        s_b = scores(k_b)
        vector_stage(s_a, k_a, False)
        value_stage(k_a)
        vector_stage(s_b, k_b, False)
        return 0
    lax.fori_loop(0, qi // 2, pair_body, 0)

    @pl.when(qi % 2 == 1)
    def _():
        s_all = scores(0)
        value_stage(TK)
        vector_stage(s_all, 0, False)
    value_stage(0)


def _fox_kernel(q_ref, k_ref, v_ref, c_ref, o_ref, qm_scr, acc_scr, p_scr, m_scr, l_scr, alpha_scr):
    qi = pl.program_id(1)
    q0 = pl.multiple_of(qi * TQ, TQ)
    krow = lax.broadcasted_iota(jnp.int32, (TK, TQ), 0)
    qcol = lax.broadcasted_iota(jnp.int32, (TK, TQ), 1)
    causal = krow <= qcol
    heads = range(N_HEADS)

    _store_masked_queries(q_ref, qm_scr)
    acc_scr[...] = jnp.zeros(acc_scr.shape, F32)
    m_scr[...] = jnp.full(m_scr.shape, NEG_INF, F32)
    l_scr[...] = jnp.zeros(l_scr.shape, F32)

    def scores(k0):
        return [_dot_nt(k_ref[pl.ds(k0, TK), _pair_cols(h)], qm_scr[h]) for h in heads]

    def vector_stage(s_all, k0, diag):
        for h in heads:
            s = s_all[h] - c_ref[pl.ds(k0, TK), h:h + 1]
            if diag:
                s = jnp.where(causal, s, NEG_INF)
            m = m_scr[h:h + 1, :]
            m_new = jnp.maximum(m, jnp.max(s, axis=0, keepdims=True))
            alpha = jnp.exp2(m - m_new)
            pr = jnp.exp2(s - m_new)
            m_scr[h:h + 1, :] = m_new
            l_scr[h:h + 1, :] = alpha * l_scr[h:h + 1, :] + jnp.sum(pr, axis=0, keepdims=True)
            alpha_scr[h:h + 1, :] = alpha
            p_scr[h] = pr.astype(BF16)

    def value_stage(k0):
        for h in heads:
            pv = _dot_tn(v_ref[pl.ds(k0, TK), _pair_cols(h)], p_scr[h])
            acc_scr[h] = alpha_scr[h:h + 1, :] * acc_scr[h] + pv

    _run_pipelined(qi, q0, scores, vector_stage, value_stage)
    for p in range(N_HEADS // 2):
        o_pair = _merge_pair(acc_scr[2 * p] / l_scr[2 * p:2 * p + 1, :],
                             acc_scr[2 * p + 1] / l_scr[2 * p + 1:2 * p + 2, :])
        o_ref[:, p * LANES:(p + 1) * LANES] = o_pair.T.astype(BF16)


def _sb_kernel(q_ref, k_ref, v_ref, tri_ref, o_ref, qm_scr, acc_scr, p_scr, r_scr):
    qi = pl.program_id(1)
    q0 = pl.multiple_of(qi * TQ, TQ)
    krow = lax.broadcasted_iota(jnp.int32, (TK, TQ), 0)
    qcol = lax.broadcasted_iota(jnp.int32, (TK, TQ), 1)
    strict = krow < qcol
    heads = range(N_HEADS)

    _store_masked_queries(q_ref, qm_scr)
    acc_scr[...] = jnp.zeros(acc_scr.shape, F32)
    r_scr[...] = jnp.zeros(r_scr.shape, F32)

    def scores(k0):
        return [_dot_nt(k_ref[pl.ds(k0, TK), _pair_cols(h)], qm_scr[h]) for h in heads]

    def vector_stage(z_all, k0, diag):
        la_all = []
        for h in heads:
            z = z_all[h]
            neg_abs = lax.bitcast_convert_type(
                lax.bitcast_convert_type(z, jnp.uint32) | jnp.uint32(0x80000000), F32)
            u = jnp.maximum(z, 0.0) + jnp.log(1.0 + jnp.exp2(neg_abs)) * LOG2E
            if diag:
                u = jnp.where(strict, u, 0.0)
            cs = _dot(tri_ref[...], u.astype(BF16))
            r = r_scr[h:h + 1, :]
            r_scr[h:h + 1, :] = r + cs[0:1, :]
            la_all.append((z - r) - cs)
        for h in heads:
            a = jnp.exp2(la_all[h])
            if diag:
                a = jnp.where(strict, a, 0.0)
            p_scr[h] = a.astype(BF16)

    def value_stage(k0):
        for h in heads:
            acc_scr[h] = acc_scr[h] + _dot_tn(v_ref[pl.ds(k0, TK), _pair_cols(h)], p_scr[h])

    _run_pipelined(qi, q0, scores, vector_stage, value_stage)
    for p in range(N_HEADS // 2):
        o_pair = _merge_pair(acc_scr[2 * p], acc_scr[2 * p + 1])
        o_ref[:, p * LANES:(p + 1) * LANES] = o_pair.T.astype(BF16)


def _fox_attention(main, c, batch, seq):
    tokens = main.shape[0]
    nq = seq // TQ
    return pl.pallas_call(
        _fox_kernel,
        grid=(batch, nq),
        in_specs=[
            pl.BlockSpec((TQ, WIDTH), lambda b, qi: (b * nq + qi, 0)),
            pl.BlockSpec((seq, WIDTH), lambda b, qi: (b, 1)),
            pl.BlockSpec((seq, WIDTH), lambda b, qi: (b, 2)),
            pl.BlockSpec((seq, LANES), lambda b, qi: (b, 0)),
        ],
        out_specs=pl.BlockSpec((TQ, WIDTH), lambda b, qi: (b * nq + qi, 0)),
        out_shape=jax.ShapeDtypeStruct((tokens, WIDTH), BF16),
        scratch_shapes=[pltpu.VMEM((N_HEADS, TQ, LANES), BF16), pltpu.VMEM((N_HEADS, LANES, TQ), F32),
                        pltpu.VMEM((N_HEADS, TK, TQ), BF16), pltpu.VMEM((N_HEADS, TQ), F32),
                        pltpu.VMEM((N_HEADS, TQ), F32), pltpu.VMEM((N_HEADS, TQ), F32)],
        compiler_params=_cparams(("arbitrary", "arbitrary")),
        name="fox_attention",
    )(main, main, main, c)


def _sb_attention(main, tri, batch, seq):
    tokens = main.shape[0]
    nq = seq // TQ
    return pl.pallas_call(
        _sb_kernel,
        grid=(batch, nq),
        in_specs=[
            pl.BlockSpec((TQ, WIDTH), lambda b, qi: (b * nq + qi, 3)),
            pl.BlockSpec((seq, WIDTH), lambda b, qi: (b, 4)),
            pl.BlockSpec((seq, WIDTH), lambda b, qi: (b, 5)),
            pl.BlockSpec((TK, TK), lambda b, qi: (0, 0)),
        ],
        out_specs=pl.BlockSpec((TQ, WIDTH), lambda b, qi: (b * nq + qi, 0)),
        out_shape=jax.ShapeDtypeStruct((tokens, WIDTH), BF16),
        scratch_shapes=[pltpu.VMEM((N_HEADS, TQ, LANES), BF16), pltpu.VMEM((N_HEADS, LANES, TQ), F32),
                        pltpu.VMEM((N_HEADS, TK, TQ), BF16), pltpu.VMEM((N_HEADS, TQ), F32)],
        compiler_params=_cparams(("arbitrary", "arbitrary")),
        name="sb_attention",
    )(main, main, main, tri)


def _mix_kernel(yc_ref, af_ref, as_ref, gate_ref, x_ref, wc_ref, wf_ref, ws_ref, wo_ref, g2_ref,
                xo_ref, hn_ref):
    d_model = x_ref.shape[1]
    y = gate_ref[:, 0:d_model].astype(F32) * _dot(yc_ref[...], wc_ref[...])
    y = y + gate_ref[:, d_model:2 * d_model].astype(F32) * _dot(af_ref[...], wf_ref[...])
    y = y + gate_ref[:, 2 * d_model:3 * d_model].astype(F32) * _dot(as_ref[...], ws_ref[...])
    xn = x_ref[...] + _dot(y.astype(BF16), wo_ref[...])
    xo_ref[...] = xn
    ms = jnp.mean(xn * xn, axis=-1, keepdims=True)
    hn_ref[...] = (xn * lax.rsqrt(ms + NORM_EPS) * g2_ref[...]).astype(BF16)


def _mix(layer, yc, af, asb, main, x2d, wc, wf, ws, wo, g2):
    tokens, d_model = x2d.shape
    gate_blk = (main.shape[1] - 3 * d_model) // (3 * d_model)
    row = lambda i: (i, 0)
    const = lambda i: (0, 0)
    stacked = lambda i: (layer, 0, 0)
    return pl.pallas_call(
        _mix_kernel,
        grid=(tokens // TM_MIX,),
        in_specs=[
            pl.BlockSpec((TM_MIX, WIDTH), row),
            pl.BlockSpec((TM_MIX, WIDTH), row),
            pl.BlockSpec((TM_MIX, WIDTH), row),
            pl.BlockSpec((TM_MIX, 3 * d_model), lambda i: (i, gate_blk)),
            pl.BlockSpec((TM_MIX, d_model), row),
            pl.BlockSpec((None, WIDTH, d_model), stacked),
            pl.BlockSpec((None, WIDTH, d_model), stacked),
            pl.BlockSpec((None, WIDTH, d_model), stacked),
            pl.BlockSpec((None, d_model, d_model), stacked),
            pl.BlockSpec((1, d_model), const),
        ],
        out_specs=[pl.BlockSpec((TM_MIX, d_model), row), pl.BlockSpec((TM_MIX, d_model), row)],
        out_shape=[jax.ShapeDtypeStruct((tokens, d_model), F32),
                   jax.ShapeDtypeStruct((tokens, d_model), BF16)],
        compiler_params=_cparams(("arbitrary",)),
        name="mix",
    )(yc, af, asb, main, x2d, wc, wf, ws, wo, g2)


def _ffn_kernel(hn_ref, wg_ref, wv_ref, cw_ref, cb_ref, wd_ref, x_ref, o_ref, ug_scr, uv_scr):
    j = pl.program_id(1)
    seq = hn_ref.shape[0]
    ug_scr[0:HALO, :] = jnp.zeros((HALO, ug_scr.shape[1]), F32)

    def run(first):
        n_chunks = seq // MM_ROWS

        def up(r):
            rows = slice(r * MM_ROWS, (r + 1) * MM_ROWS)
            base = r * MM_ROWS + HALO
            ug_scr[base:base + MM_ROWS, :] = _dot(hn_ref[rows, :], wg_ref[...])
            uv_scr[rows, :] = _dot(hn_ref[rows, :], wv_ref[...])

        def down(r):
            rows = slice(r * MM_ROWS, (r + 1) * MM_ROWS)
            base = r * MM_ROWS + HALO
            u = cw_ref[2:3, :] * ug_scr[base:base + MM_ROWS, :] + cb_ref[...]
            u = u + cw_ref[1:2, :] * ug_scr[base - 1:base - 1 + MM_ROWS, :]
            u = u + cw_ref[0:1, :] * ug_scr[base - 2:base - 2 + MM_ROWS, :]
            part = _dot((u * _sigmoid(u) * uv_scr[rows, :]).astype(BF16), wd_ref[...])
            o_ref[rows, :] = (x_ref[rows, :] if first else o_ref[rows, :]) + part

        up(0)
        for r in range(n_chunks):
            if r + 1 < n_chunks:
                up(r + 1)
            down(r)

    @pl.when(j == 0)
    def _():
        run(True)

    @pl.when(j > 0)
    def _():
        run(False)


def _ffn(layer, hn2, w_up, conv_w, conv_b, w_down, x2d, batch, seq):
    tokens, d_model = hn2.shape
    d_ff = w_up.shape[2] // 2
    nj = d_ff // TN_UP
    return pl.pallas_call(
        _ffn_kernel,
        grid=(batch, nj),
        in_specs=[
            pl.BlockSpec((seq, d_model), lambda b, j: (b, 0)),
            pl.BlockSpec((None, d_model, TN_UP), lambda b, j: (layer, 0, j)),
            pl.BlockSpec((None, d_model, TN_UP), lambda b, j: (layer, 0, nj + j)),
            pl.BlockSpec((CONV_K, TN_UP), lambda b, j: (0, j)),
            pl.BlockSpec((1, TN_UP), lambda b, j: (0, j)),
            pl.BlockSpec((None, TN_UP, d_model), lambda b, j: (layer, j, 0)),
            pl.BlockSpec((seq, d_model), lambda b, j: (b, 0)),
        ],
        out_specs=pl.BlockSpec((seq, d_model), lambda b, j: (b, 0)),
        out_shape=jax.ShapeDtypeStruct((tokens, d_model), F32),
        scratch_shapes=[pltpu.VMEM((seq + HALO, TN_UP), F32), pltpu.VMEM((seq, TN_UP), F32)],
        compiler_params=_cparams(("arbitrary", "arbitrary")),
        name="ffn",
    )(hn2, w_up, w_up, conv_w, conv_b, w_down, x2d)


def kernel(x, norm1_g, w_in, fox_f_bias, gate_bias, conv_w, fox_q_norm_g, fox_k_norm_g, w_proj_conv,
           w_proj_fox, w_proj_sb, w_out, norm2_g, w_up, ffn_conv_w, ffn_conv_b, w_down):
    batch, seq, d_model = x.shape
    depth = w_in.shape[0]
    scale = HEAD_DIM ** -0.5 * LOG2E
    assert seq % max(TQ, TK, CUM_BLK, ROW_CHUNK) == 0 and TQ == TK

    o_f = 6 * WIDTH
    o_sb = o_f + N_HEADS
    w_a = w_in[:, :, :o_f].astype(BF16)
    w_b = w_in[:, :, o_sb:].astype(BF16)
    w_f = jnp.pad(w_in[:, :, o_f:o_sb], ((0, 0), (0, 0), (0, LANES - N_HEADS))).astype(BF16)
    f_bias = jnp.pad(fox_f_bias, ((0, 0), (0, LANES - N_HEADS)))[:, None, :]
    qk_gain = jnp.concatenate([jnp.tile(fox_q_norm_g * scale, (1, N_HEADS)),
                               jnp.tile(fox_k_norm_g, (1, N_HEADS))], axis=-1)[:, None, :]
    wc, wf, ws, wo = (w.astype(BF16) for w in (w_proj_conv, w_proj_fox, w_proj_sb, w_out))
    w_up_b, w_down_b = w_up.astype(BF16), w_down.astype(BF16)

    idx = jnp.arange(CUM_BLK)
    tri_incl = (idx[None, :] <= idx[:, None]).astype(BF16)
    tri_suffix = (idx[None, :] >= idx[:, None]).astype(BF16)

    x2d = x.reshape(batch * seq, d_model)
    for l in range(depth):
        main, yc, c = _in_proj(l, x2d, norm1_g[l][None, :], w_a, w_b, w_f, gate_bias[l][None, :], f_bias[l],
                               conv_w[l], qk_gain[l], tri_incl, batch, seq, scale)
        af = _fox_attention(main, c, batch, seq)
        asb = _sb_attention(main, tri_suffix, batch, seq)
        x2d, hn2 = _mix(l, yc, af, asb, main, x2d, wc, wf, ws, wo, norm2_g[l][None, :])
        x2d = _ffn(l, hn2, w_up_b, ffn_conv_w[l], ffn_conv_b[l][None, :], w_down_b, x2d, batch, seq)
    return x2d.reshape(batch, seq, d_model)
```

```python
import functools

import jax
import jax.numpy as jnp
from jax import lax
from jax.experimental import pallas as pl
from jax.experimental.pallas import tpu as pltpu

F32 = jnp.float32
BF16 = jnp.bfloat16

HEAD_DIM = 64
N_HEADS = 8
WIDTH = N_HEADS * HEAD_DIM
CONV_K = 3
NORM_EPS = 1e-6
NEG_INF = -1e30
LOG2E = 1.4426950408889634

LANES = 128
SUBLANES = 8
VMEM_LIMIT = 56 * 1024 * 1024

TN_IN = 512
ROW_CHUNK = 256
MM_ROWS = 512
CUM_BLK = 256
TQ = 256
TK = 256
TM_MIX = 1024
TN_UP = 256
HALO = SUBLANES


def _cparams(sem):
    return pltpu.CompilerParams(dimension_semantics=sem, vmem_limit_bytes=VMEM_LIMIT)


def _log_sigmoid(v):
    return jnp.minimum(v, 0.0) - jnp.log(1.0 + jnp.exp(-jnp.abs(v)))


def _split3(v):
    hi = v.astype(BF16)
    r1 = v - hi.astype(F32)
    mid = r1.astype(BF16)
    lo = (r1 - mid.astype(F32)).astype(BF16)
    return hi, mid, lo


def _dot(a, b):
    return jnp.dot(a, b, preferred_element_type=F32)


def _dot_nt(a, b):
    return lax.dot_general(a, b, (((1,), (1,)), ((), ())), preferred_element_type=F32)


def _dot_tn(a, b):
    return lax.dot_general(a, b, (((0,), (0,)), ((), ())), preferred_element_type=F32)


J_C, J_H, J_B, J_FQ, J_FK, J_FV, J_SQ, J_GATE = 0, 1, 2, 3, 4, 5, 6, 9


def _first_view_tile(j):
    return jnp.where(j < J_FQ, (j + 1) % 3, jnp.minimum(j, J_FV))


def _sigmoid(v):
    return 0.5 * jnp.tanh(0.5 * v) + 0.5


def _in_kernel(x_ref, g1_ref, wa_ref, wb_ref, wf_ref, bias_ref, fb_ref, cw_ref, qkg_ref, tri_ref,
               main_ref, yc_ref, c_ref, hn_scr, ch_scr, *, sb_scale):
    j = pl.program_id(1)
    seq = x_ref.shape[0]

    def tile(epilogue, w_ref):
        for r in range(seq // MM_ROWS):
            rows = slice(r * MM_ROWS, (r + 1) * MM_ROWS)
            epilogue(r, rows, _dot(hn_scr[rows, :], w_ref[...]))

    def store_c(r, rows, acc):
        ch_scr[HALO + r * MM_ROWS:HALO + (r + 1) * MM_ROWS, :] = acc

    @pl.when(j == J_C)
    def _():
        def norm_body(r, _):
            rows = pl.ds(pl.multiple_of(r * ROW_CHUNK, ROW_CHUNK), ROW_CHUNK)
            x = x_ref[rows, :]
            ms = jnp.mean(x * x, axis=-1, keepdims=True)
            hn_scr[rows, :] = (x * lax.rsqrt(ms + NORM_EPS) * g1_ref[...]).astype(BF16)
            return 0
        lax.fori_loop(0, seq // ROW_CHUNK, norm_body, 0)
        ch_scr[0:HALO, :] = jnp.zeros((HALO, ch_scr.shape[1]), F32)
        tile(store_c, wa_ref)

        blocks = [slice(blk * CUM_BLK, (blk + 1) * CUM_BLK) for blk in range(seq // CUM_BLK)]
        f_all = [_dot(hn_scr[rows, :], wf_ref[...]) for rows in blocks]
        parts = [_split3(_log_sigmoid(f + fb_ref[...]) * LOG2E) for f in f_all]
        carry = jnp.zeros((1, LANES), F32)
        for rows, (hi, mid, lo) in zip(blocks, parts):
            tri = tri_ref[...]
            cs = _dot(tri, hi) + _dot(tri, mid) + _dot(tri, lo) + carry
            c_ref[rows, :] = cs
            carry = cs[CUM_BLK - 1:CUM_BLK, :]

    @pl.when(j == J_H)
    def _():
        def fn(r, rows, acc):
            dst = slice(HALO + r * MM_ROWS, HALO + (r + 1) * MM_ROWS)
            ch_scr[dst, :] = ch_scr[dst, :] * acc
        tile(fn, wa_ref)

    @pl.when(j == J_B)
    def _():
        def fn(r, rows, acc):
            base = r * MM_ROWS + HALO
            u = cw_ref[2:3, :] * ch_scr[base:base + MM_ROWS, :]
            u = u + cw_ref[1:2, :] * ch_scr[base - 1:base - 1 + MM_ROWS, :]
            u = u + cw_ref[0:1, :] * ch_scr[base - 2:base - 2 + MM_ROWS, :]
            yc_ref[rows, :] = (acc * u).astype(BF16)
        tile(fn, wa_ref)

    @pl.when((j == J_FQ) | (j == J_FK))
    def _():
        lane = lax.broadcasted_iota(jnp.int32, (MM_ROWS, LANES), 1)
        low = lane < HEAD_DIM

        def fn(r, rows, acc):
            for cidx in range(TN_IN // LANES):
                cols = slice(cidx * LANES, (cidx + 1) * LANES)
                a = acc[:, cols]
                a2 = a * a
                s_lo = jnp.sum(jnp.where(low, a2, 0.0), axis=-1, keepdims=True)
                s_hi = jnp.sum(jnp.where(low, 0.0, a2), axis=-1, keepdims=True)
                ms = jnp.where(low, s_lo, s_hi) * (1.0 / HEAD_DIM)
                main_ref[rows, cols] = (a * lax.rsqrt(ms + NORM_EPS) * qkg_ref[:, cols]).astype(BF16)
        tile(fn, wa_ref)

    def store_raw(r, rows, acc):
        main_ref[rows, :] = acc.astype(BF16)

    @pl.when(j == J_FV)
    def _():
        tile(store_raw, wa_ref)

    @pl.when(j == J_SQ)
    def _():
        def fn(r, rows, acc):
            main_ref[rows, :] = (acc * sb_scale).astype(BF16)
        tile(fn, wb_ref)

    @pl.when((j > J_SQ) & (j < J_GATE))
    def _():
        tile(store_raw, wb_ref)

    @pl.when(j >= J_GATE)
    def _():
        def fn(r, rows, acc):
            main_ref[rows, :] = _sigmoid(acc + bias_ref[...]).astype(BF16)
        tile(fn, wb_ref)


def _in_proj(layer, x2d, g1, w_a, w_b, w_f, gate_bias, f_bias, conv_w, qk_gain, tri, batch, seq, sb_scale):
    d_model = x2d.shape[1]
    assert w_b.shape[2] % TN_IN == 0 and gate_bias.shape[1] % TN_IN == 0
    nj = J_SQ + w_b.shape[2] // TN_IN
    n_out = (nj - J_FQ) * TN_IN
    tokens = x2d.shape[0]
    return pl.pallas_call(
        functools.partial(_in_kernel, sb_scale=sb_scale),
        grid=(batch, nj),
        in_specs=[
            pl.BlockSpec((seq, d_model), lambda i, j: (i, 0)),
            pl.BlockSpec((1, d_model), lambda i, j: (0, 0)),
            pl.BlockSpec((None, d_model, TN_IN), lambda i, j: (layer, 0, _first_view_tile(j))),
            pl.BlockSpec((None, d_model, TN_IN), lambda i, j: (layer, 0, jnp.maximum(j - J_SQ, 0))),
            pl.BlockSpec((None, d_model, LANES), lambda i, j: (layer, 0, 0)),
            pl.BlockSpec((1, TN_IN), lambda i, j: (0, jnp.maximum(j - J_GATE, 0))),
            pl.BlockSpec((1, LANES), lambda i, j: (0, 0)),
            pl.BlockSpec((CONV_K, WIDTH), lambda i, j: (0, 0)),
            pl.BlockSpec((1, TN_IN), lambda i, j: (0, jnp.clip(j - J_FQ, 0, 1))),
            pl.BlockSpec((CUM_BLK, CUM_BLK), lambda i, j: (0, 0)),
        ],
        out_specs=[
            pl.BlockSpec((seq, TN_IN), lambda i, j: (i, jnp.maximum(j - J_FQ, 0))),
            pl.BlockSpec((seq, WIDTH), lambda i, j: (i, 0)),
            pl.BlockSpec((seq, LANES), lambda i, j: (i, 0)),
        ],
        out_shape=[
            jax.ShapeDtypeStruct((tokens, n_out), BF16),
            jax.ShapeDtypeStruct((tokens, WIDTH), BF16),
            jax.ShapeDtypeStruct((tokens, LANES), F32),
        ],
        scratch_shapes=[
            pltpu.VMEM((seq, d_model), BF16),
            pltpu.VMEM((seq + HALO, WIDTH), F32),
        ],
        compiler_params=_cparams(("arbitrary", "arbitrary")),
        name="in_proj",
    )(x2d, g1, w_a, w_b, w_f, gate_bias, f_bias, conv_w, qk_gain, tri)


def _pair_masks(q_pair):
    lane = lax.broadcasted_iota(jnp.int32, q_pair.shape, 1)
    zero = jnp.zeros_like(q_pair)
    return (jnp.where(lane < HEAD_DIM, q_pair, zero), jnp.where(lane < HEAD_DIM, zero, q_pair))


def _merge_pair(o0, o1):
    return jnp.concatenate([o0[0:HEAD_DIM, :], o1[HEAD_DIM:2 * HEAD_DIM, :]], axis=0)


def _store_masked_queries(q_ref, qm_scr):
    for p in range(N_HEADS // 2):
        qm = _pair_masks(q_ref[:, p * LANES:(p + 1) * LANES])
        qm_scr[2 * p] = qm[0]
        qm_scr[2 * p + 1] = qm[1]


def _pair_cols(h):
    return slice((h // 2) * LANES, (h // 2 + 1) * LANES)


def _run_pipelined(qi, q0, scores, vector_stage, value_stage):
    vector_stage(scores(q0), q0, True)

    def pair_body(i, _):
        k_a = pl.multiple_of((qi - 1 - 2 * i) * TK, TK)
        k_b = pl.multiple_of(k_a - TK, TK)
        s_a = scores(k_a)
        value_stage(pl.multiple_of(k_a + TK, TK))
        s_b = scores(k_b)
        vector_stage(s_a, k_a, False)
        value_stage(k_a)
        vector_stage(s_b, k_b, False)
        return 0
    lax.fori_loop(0, qi // 2, pair_body, 0)

    @pl.when(qi % 2 == 1)
    def _():
        s_all = scores(0)
        value_stage(TK)
        vector_stage(s_all, 0, False)
    value_stage(0)


def _fox_kernel(q_ref, k_ref, v_ref, c_ref, o_ref, qm_scr, acc_scr, p_scr, m_scr, l_scr, alpha_scr):
    qi = pl.program_id(1)
    q0 = pl.multiple_of(qi * TQ, TQ)
    krow = lax.broadcasted_iota(jnp.int32, (TK, TQ), 0)
    qcol = lax.broadcasted_iota(jnp.int32, (TK, TQ), 1)
    causal = krow <= qcol
    heads = range(N_HEADS)

    _store_masked_queries(q_ref, qm_scr)
    acc_scr[...] = jnp.zeros(acc_scr.shape, F32)
    m_scr[...] = jnp.full(m_scr.shape, NEG_INF, F32)
    l_scr[...] = jnp.zeros(l_scr.shape, F32)

    def scores(k0):
        return [_dot_nt(k_ref[pl.ds(k0, TK), _pair_cols(h)], qm_scr[h]) for h in heads]

    def vector_stage(s_all, k0, diag):
        for h in heads:
            s = s_all[h] - c_ref[pl.ds(k0, TK), h:h + 1]
            if diag:
                s = jnp.where(causal, s, NEG_INF)
            m = m_scr[h:h + 1, :]
            m_new = jnp.maximum(m, jnp.max(s, axis=0, keepdims=True))
            alpha = jnp.exp2(m - m_new)
            pr = jnp.exp2(s - m_new)
            m_scr[h:h + 1, :] = m_new
            l_scr[h:h + 1, :] = alpha * l_scr[h:h + 1, :] + jnp.sum(pr, axis=0, keepdims=True)
            alpha_scr[h:h + 1, :] = alpha
            p_scr[h] = pr.astype(BF16)

    def value_stage(k0):
        for h in heads:
            pv = _dot_tn(v_ref[pl.ds(k0, TK), _pair_cols(h)], p_scr[h])
            acc_scr[h] = alpha_scr[h:h + 1, :] * acc_scr[h] + pv

    _run_pipelined(qi, q0, scores, vector_stage, value_stage)
    for p in range(N_HEADS // 2):
        o_pair = _merge_pair(acc_scr[2 * p] / l_scr[2 * p:2 * p + 1, :],
                             acc_scr[2 * p + 1] / l_scr[2 * p + 1:2 * p + 2, :])
        o_ref[:, p * LANES:(p + 1) * LANES] = o_pair.T.astype(BF16)


def _sb_kernel(q_ref, k_ref, v_ref, tri_ref, o_ref, qm_scr, acc_scr, p_scr, r_scr):
    qi = pl.program_id(1)
    q0 = pl.multiple_of(qi * TQ, TQ)
    krow = lax.broadcasted_iota(jnp.int32, (TK, TQ), 0)
    qcol = lax.broadcasted_iota(jnp.int32, (TK, TQ), 1)
    strict = krow < qcol
    heads = range(N_HEADS)

    _store_masked_queries(q_ref, qm_scr)
    acc_scr[...] = jnp.zeros(acc_scr.shape, F32)
    r_scr[...] = jnp.zeros(r_scr.shape, F32)

    def scores(k0):
        return [_dot_nt(k_ref[pl.ds(k0, TK), _pair_cols(h)], qm_scr[h]) for h in heads]

    def vector_stage(z_all, k0, diag):
        la_all = []
        for h in heads:
            z = z_all[h]
            neg_abs = lax.bitcast_convert_type(
                lax.bitcast_convert_type(z, jnp.uint32) | jnp.uint32(0x80000000), F32)
            u = jnp.maximum(z, 0.0) + jnp.log(1.0 + jnp.exp2(neg_abs)) * LOG2E
            if diag:
                u = jnp.where(strict, u, 0.0)
            cs = _dot(tri_ref[...], u.astype(BF16))
            r = r_scr[h:h + 1, :]
            r_scr[h:h + 1, :] = r + cs[0:1, :]
            la_all.append((z - r) - cs)
        for h in heads:
            a = jnp.exp2(la_all[h])
            if diag:
                a = jnp.where(strict, a, 0.0)
            p_scr[h] = a.astype(BF16)

    def value_stage(k0):
        for h in heads:
            acc_scr[h] = acc_scr[h] + _dot_tn(v_ref[pl.ds(k0, TK), _pair_cols(h)], p_scr[h])

    _run_pipelined(qi, q0, scores, vector_stage, value_stage)
    for p in range(N_HEADS // 2):
        o_pair = _merge_pair(acc_scr[2 * p], acc_scr[2 * p + 1])
        o_ref[:, p * LANES:(p + 1) * LANES] = o_pair.T.astype(BF16)


def _fox_attention(main, c, batch, seq):
    tokens = main.shape[0]
    nq = seq // TQ
    return pl.pallas_call(
        _fox_kernel,
        grid=(batch, nq),
        in_specs=[
            pl.BlockSpec((TQ, WIDTH), lambda b, qi: (b * nq + qi, 0)),
            pl.BlockSpec((seq, WIDTH), lambda b, qi: (b, 1)),
            pl.BlockSpec((seq, WIDTH), lambda b, qi: (b, 2)),
            pl.BlockSpec((seq, LANES), lambda b, qi: (b, 0)),
        ],
        out_specs=pl.BlockSpec((TQ, WIDTH), lambda b, qi: (b * nq + qi, 0)),
        out_shape=jax.ShapeDtypeStruct((tokens, WIDTH), BF16),
        scratch_shapes=[pltpu.VMEM((N_HEADS, TQ, LANES), BF16), pltpu.VMEM((N_HEADS, LANES, TQ), F32),
                        pltpu.VMEM((N_HEADS, TK, TQ), BF16), pltpu.VMEM((N_HEADS, TQ), F32),
                        pltpu.VMEM((N_HEADS, TQ), F32), pltpu.VMEM((N_HEADS, TQ), F32)],
        compiler_params=_cparams(("arbitrary", "arbitrary")),
        name="fox_attention",
    )(main, main, main, c)


def _sb_attention(main, tri, batch, seq):
    tokens = main.shape[0]
    nq = seq // TQ
    return pl.pallas_call(
        _sb_kernel,
        grid=(batch, nq),
        in_specs=[
            pl.BlockSpec((TQ, WIDTH), lambda b, qi: (b * nq + qi, 3)),
            pl.BlockSpec((seq, WIDTH), lambda b, qi: (b, 4)),
            pl.BlockSpec((seq, WIDTH), lambda b, qi: (b, 5)),
            pl.BlockSpec((TK, TK), lambda b, qi: (0, 0)),
        ],
        out_specs=pl.BlockSpec((TQ, WIDTH), lambda b, qi: (b * nq + qi, 0)),
        out_shape=jax.ShapeDtypeStruct((tokens, WIDTH), BF16),
        scratch_shapes=[pltpu.VMEM((N_HEADS, TQ, LANES), BF16), pltpu.VMEM((N_HEADS, LANES, TQ), F32),
                        pltpu.VMEM((N_HEADS, TK, TQ), BF16), pltpu.VMEM((N_HEADS, TQ), F32)],
        compiler_params=_cparams(("arbitrary", "arbitrary")),
        name="sb_attention",
    )(main, main, main, tri)


def _mix_kernel(yc_ref, af_ref, as_ref, gate_ref, x_ref, wc_ref, wf_ref, ws_ref, wo_ref, g2_ref,
                xo_ref, hn_ref, y_scr):
    d_model = x_ref.shape[1]
    chunks = [slice(r * MM_ROWS, (r + 1) * MM_ROWS) for r in range(x_ref.shape[0] // MM_ROWS)]
    for rows in chunks:
        y = gate_ref[rows, 0:d_model].astype(F32) * _dot(yc_ref[rows, :], wc_ref[...])
        y = y + gate_ref[rows, d_model:2 * d_model].astype(F32) * _dot(af_ref[rows, :], wf_ref[...])
        y = y + gate_ref[rows, 2 * d_model:3 * d_model].astype(F32) * _dot(as_ref[rows, :], ws_ref[...])
        y_scr[rows, :] = y.astype(BF16)
    for rows in chunks:
        xn = x_ref[rows, :] + _dot(y_scr[rows, :], wo_ref[...])
        xo_ref[rows, :] = xn
        ms = jnp.mean(xn * xn, axis=-1, keepdims=True)
        hn_ref[rows, :] = (xn * lax.rsqrt(ms + NORM_EPS) * g2_ref[...]).astype(BF16)


def _mix(layer, yc, af, asb, main, x2d, wc, wf, ws, wo, g2):
    tokens, d_model = x2d.shape
    gate_blk = (main.shape[1] - 3 * d_model) // (3 * d_model)
    row = lambda i: (i, 0)
    const = lambda i: (0, 0)
    stacked = lambda i: (layer, 0, 0)
    return pl.pallas_call(
        _mix_kernel,
        grid=(tokens // TM_MIX,),
        in_specs=[
            pl.BlockSpec((TM_MIX, WIDTH), row),
            pl.BlockSpec((TM_MIX, WIDTH), row),
            pl.BlockSpec((TM_MIX, WIDTH), row),
            pl.BlockSpec((TM_MIX, 3 * d_model), lambda i: (i, gate_blk)),
            pl.BlockSpec((TM_MIX, d_model), row),
            pl.BlockSpec((None, WIDTH, d_model), stacked),
            pl.BlockSpec((None, WIDTH, d_model), stacked),
            pl.BlockSpec((None, WIDTH, d_model), stacked),
            pl.BlockSpec((None, d_model, d_model), stacked),
            pl.BlockSpec((1, d_model), const),
        ],
        out_specs=[pl.BlockSpec((TM_MIX, d_model), row), pl.BlockSpec((TM_MIX, d_model), row)],
        out_shape=[jax.ShapeDtypeStruct((tokens, d_model), F32),
                   jax.ShapeDtypeStruct((tokens, d_model), BF16)],
        scratch_shapes=[pltpu.VMEM((TM_MIX, d_model), BF16)],
        compiler_params=_cparams(("arbitrary",)),
        name="mix",
    )(yc, af, asb, main, x2d, wc, wf, ws, wo, g2)


def _ffn_kernel(hn_ref, wg_ref, wv_ref, cw_ref, cb_ref, wd_ref, x_ref, o_ref, ug_scr, uv_scr):
    j = pl.program_id(1)
    seq = hn_ref.shape[0]
    ug_scr[0:HALO, :] = jnp.zeros((HALO, ug_scr.shape[1]), F32)

    def run(first):
        n_chunks = seq // MM_ROWS

        def up(r):
            rows = slice(r * MM_ROWS, (r + 1) * MM_ROWS)
            base = r * MM_ROWS + HALO
            ug_scr[base:base + MM_ROWS, :] = _dot(hn_ref[rows, :], wg_ref[...])
            uv_scr[rows, :] = _dot(hn_ref[rows, :], wv_ref[...])

        def down(r):
            rows = slice(r * MM_ROWS, (r + 1) * MM_ROWS)
            base = r * MM_ROWS + HALO
            u = cw_ref[2:3, :] * ug_scr[base:base + MM_ROWS, :] + cb_ref[...]
            u = u + cw_ref[1:2, :] * ug_scr[base - 1:base - 1 + MM_ROWS, :]
            u = u + cw_ref[0:1, :] * ug_scr[base - 2:base - 2 + MM_ROWS, :]
            part = _dot((u * _sigmoid(u) * uv_scr[rows, :]).astype(BF16), wd_ref[...])
            o_ref[rows, :] = (x_ref[rows, :] if first else o_ref[rows, :]) + part

        up(0)
        for r in range(n_chunks):
            if r + 1 < n_chunks:
                up(r + 1)
            down(r)

    @pl.when(j == 0)
    def _():
        run(True)

    @pl.when(j > 0)
    def _():
        run(False)


def _ffn(layer, hn2, w_up, conv_w, conv_b, w_down, x2d, batch, seq):
    tokens, d_model = hn2.shape
    d_ff = w_up.shape[2] // 2
    nj = d_ff // TN_UP
    return pl.pallas_call(
        _ffn_kernel,
        grid=(batch, nj),
        in_specs=[
            pl.BlockSpec((seq, d_model), lambda b, j: (b, 0)),
            pl.BlockSpec((None, d_model, TN_UP), lambda b, j: (layer, 0, j)),
            pl.BlockSpec((None, d_model, TN_UP), lambda b, j: (layer, 0, nj + j)),
            pl.BlockSpec((CONV_K, TN_UP), lambda b, j: (0, j)),
            pl.BlockSpec((1, TN_UP), lambda b, j: (0, j)),
            pl.BlockSpec((None, TN_UP, d_model), lambda b, j: (layer, j, 0)),
            pl.BlockSpec((seq, d_model), lambda b, j: (b, 0)),
        ],
        out_specs=pl.BlockSpec((seq, d_model), lambda b, j: (b, 0)),
        out_shape=jax.ShapeDtypeStruct((tokens, d_model), F32),
        scratch_shapes=[pltpu.VMEM((seq + HALO, TN_UP), F32), pltpu.VMEM((seq, TN_UP), F32)],
        compiler_params=_cparams(("arbitrary", "arbitrary")),
        name="ffn",
    )(hn2, w_up, w_up, conv_w, conv_b, w_down, x2d)


def kernel(x, norm1_g, w_in, fox_f_bias, gate_bias, conv_w, fox_q_norm_g, fox_k_norm_g, w_proj_conv,
           w_proj_fox, w_proj_sb, w_out, norm2_g, w_up, ffn_conv_w, ffn_conv_b, w_down):
    batch, seq, d_model = x.shape
    depth = w_in.shape[0]
    scale = HEAD_DIM ** -0.5 * LOG2E
    assert seq % max(TQ, TK, CUM_BLK, ROW_CHUNK) == 0 and TQ == TK

    o_f = 6 * WIDTH
    o_sb = o_f + N_HEADS
    w_a = w_in.astype(BF16)
    w_b = w_a[:, :, o_sb:]
    w_f = jnp.pad(w_a[:, :, o_f:o_sb], ((0, 0), (0, 0), (0, LANES - N_HEADS)))
    f_bias = jnp.pad(fox_f_bias, ((0, 0), (0, LANES - N_HEADS)))[:, None, :]
    qk_gain = jnp.concatenate([jnp.tile(fox_q_norm_g * scale, (1, N_HEADS)),
                               jnp.tile(fox_k_norm_g, (1, N_HEADS))], axis=-1)[:, None, :]
    wc, wf, ws, wo = (w.astype(BF16) for w in (w_proj_conv, w_proj_fox, w_proj_sb, w_out))
    w_up_b, w_down_b = w_up.astype(BF16), w_down.astype(BF16)

    idx = jnp.arange(CUM_BLK)
    tri_incl = (idx[None, :] <= idx[:, None]).astype(BF16)
    tri_suffix = (idx[None, :] >= idx[:, None]).astype(BF16)

    x2d = x.reshape(batch * seq, d_model)
    for l in range(depth):
        main, yc, c = _in_proj(l, x2d, norm1_g[l][None, :], w_a, w_b, w_f, gate_bias[l][None, :], f_bias[l],
                               conv_w[l], qk_gain[l], tri_incl, batch, seq, scale)
        af = _fox_attention(main, c, batch, seq)
        asb = _sb_attention(main, tri_suffix, batch, seq)
        x2d, hn2 = _mix(l, yc, af, asb, main, x2d, wc, wf, ws, wo, norm2_g[l][None, :])
        x2d = _ffn(l, hn2, w_up_b, ffn_conv_w[l], ffn_conv_b[l][None, :], w_down_b, x2d, batch, seq)
    return x2d.reshape(batch, seq, d_model)
```

```python
import functools

import jax
import jax.numpy as jnp
from jax import lax
from jax.experimental import pallas as pl
from jax.experimental.pallas import tpu as pltpu

F32 = jnp.float32
BF16 = jnp.bfloat16

HEAD_DIM = 64
N_HEADS = 8
WIDTH = N_HEADS * HEAD_DIM
CONV_K = 3
NORM_EPS = 1e-6
NEG_INF = -1e30
LOG2E = 1.4426950408889634

LANES = 128
SUBLANES = 8
VMEM_LIMIT = 56 * 1024 * 1024

TN_IN = 512
ROW_CHUNK = 256
MM_ROWS = 512
CUM_BLK = 256
TQ = 256
TK = 256
TM_MIX = 1024
TN_UP = 256
HALO = SUBLANES


def _cparams(sem):
    return pltpu.CompilerParams(dimension_semantics=sem, vmem_limit_bytes=VMEM_LIMIT)


def _log_sigmoid(v):
    return jnp.minimum(v, 0.0) - jnp.log(1.0 + jnp.exp(-jnp.abs(v)))


def _split3(v):
    hi = v.astype(BF16)
    r1 = v - hi.astype(F32)
    mid = r1.astype(BF16)
    lo = (r1 - mid.astype(F32)).astype(BF16)
    return hi, mid, lo


def _dot(a, b):
    return jnp.dot(a, b, preferred_element_type=F32)


def _dot_nt(a, b):
    return lax.dot_general(a, b, (((1,), (1,)), ((), ())), preferred_element_type=F32)


def _dot_tn(a, b):
    return lax.dot_general(a, b, (((0,), (0,)), ((), ())), preferred_element_type=F32)


S_CONV, S_FOX, S_SB, S_GATE = 0, 1, 2, 3
SUBTILES = 3


def _sigmoid(v):
    return 0.5 * jnp.tanh(0.5 * v) + 0.5


def _in_kernel(x_ref, g1_ref, wa_ref, wb_ref, wf_ref, bias_ref, fb_ref, cw_ref, qkg_ref, tri_ref,
               main_ref, yc_ref, c_ref, hn_scr, ch_scr, *, sb_scale):
    j = pl.program_id(1)
    seq = x_ref.shape[0]

    def tile(epilogue, w_ref, t):
        cols = slice(t * TN_IN, (t + 1) * TN_IN)
        for r in range(seq // MM_ROWS):
            rows = slice(r * MM_ROWS, (r + 1) * MM_ROWS)
            epilogue(r, rows, cols, _dot(hn_scr[rows, :], w_ref[:, cols]))

    def store_raw(r, rows, cols, acc):
        main_ref[rows, cols] = acc.astype(BF16)

    @pl.when(j == S_CONV)
    def _():
        def norm_body(r, _):
            rows = pl.ds(pl.multiple_of(r * ROW_CHUNK, ROW_CHUNK), ROW_CHUNK)
            x = x_ref[rows, :]
            ms = jnp.mean(x * x, axis=-1, keepdims=True)
            hn_scr[rows, :] = (x * lax.rsqrt(ms + NORM_EPS) * g1_ref[...]).astype(BF16)
            return 0
        lax.fori_loop(0, seq // ROW_CHUNK, norm_body, 0)
        ch_scr[0:HALO, :] = jnp.zeros((HALO, ch_scr.shape[1]), F32)

        def store_c(r, rows, cols, acc):
            ch_scr[HALO + r * MM_ROWS:HALO + (r + 1) * MM_ROWS, :] = acc

        def times_h(r, rows, cols, acc):
            dst = slice(HALO + r * MM_ROWS, HALO + (r + 1) * MM_ROWS)
            ch_scr[dst, :] = ch_scr[dst, :] * acc

        def conv_times_b(r, rows, cols, acc):
            base = r * MM_ROWS + HALO
            u = cw_ref[2:3, :] * ch_scr[base:base + MM_ROWS, :]
            u = u + cw_ref[1:2, :] * ch_scr[base - 1:base - 1 + MM_ROWS, :]
            u = u + cw_ref[0:1, :] * ch_scr[base - 2:base - 2 + MM_ROWS, :]
            yc_ref[rows, :] = (acc * u).astype(BF16)

        tile(store_c, wa_ref, 1)
        tile(times_h, wa_ref, 2)
        tile(conv_times_b, wa_ref, 0)

        blocks = [slice(blk * CUM_BLK, (blk + 1) * CUM_BLK) for blk in range(seq // CUM_BLK)]
        f_all = [_dot(hn_scr[rows, :], wf_ref[...]) for rows in blocks]
        parts = [_split3(_log_sigmoid(f + fb_ref[...]) * LOG2E) for f in f_all]
        carry = jnp.zeros((1, LANES), F32)
        for rows, (hi, mid, lo) in zip(blocks, parts):
            tri = tri_ref[...]
            cs = _dot(tri, hi) + _dot(tri, mid) + _dot(tri, lo) + carry
            c_ref[rows, :] = cs
            carry = cs[CUM_BLK - 1:CUM_BLK, :]

    @pl.when(j == S_FOX)
    def _():
        lane = lax.broadcasted_iota(jnp.int32, (MM_ROWS, LANES), 1)
        low = lane < HEAD_DIM

        def head_norm(r, rows, cols, acc):
            for cidx in range(TN_IN // LANES):
                sub = slice(cols.start + cidx * LANES, cols.start + (cidx + 1) * LANES)
                a = acc[:, cidx * LANES:(cidx + 1) * LANES]
                a2 = a * a
                s_lo = jnp.sum(jnp.where(low, a2, 0.0), axis=-1, keepdims=True)
                s_hi = jnp.sum(jnp.where(low, 0.0, a2), axis=-1, keepdims=True)
                ms = jnp.where(low, s_lo, s_hi) * (1.0 / HEAD_DIM)
                main_ref[rows, sub] = (a * lax.rsqrt(ms + NORM_EPS) * qkg_ref[:, sub]).astype(BF16)
        tile(head_norm, wa_ref, 0)
        tile(head_norm, wa_ref, 1)
        tile(store_raw, wa_ref, 2)

    @pl.when(j == S_SB)
    def _():
        def scaled(r, rows, cols, acc):
            main_ref[rows, cols] = (acc * sb_scale).astype(BF16)
        tile(scaled, wb_ref, 0)
        tile(store_raw, wb_ref, 1)
        tile(store_raw, wb_ref, 2)

    @pl.when(j >= S_GATE)
    def _():
        def gate(r, rows, cols, acc):
            main_ref[rows, cols] = _sigmoid(acc + bias_ref[:, cols]).astype(BF16)
        for t in range(SUBTILES):
            tile(gate, wb_ref, t)


def _in_proj(layer, x2d, g1, w_a, w_b, w_f, gate_bias, f_bias, conv_w, qk_gain, tri, batch, seq, sb_scale):
    d_model = x2d.shape[1]
    step = SUBTILES * TN_IN
    assert w_b.shape[2] % step == 0 and gate_bias.shape[1] % step == 0 and qk_gain.shape[1] == 2 * TN_IN
    nj = S_SB + w_b.shape[2] // step
    n_out = (nj - S_FOX) * step
    tokens = x2d.shape[0]
    return pl.pallas_call(
        functools.partial(_in_kernel, sb_scale=sb_scale),
        grid=(batch, nj),
        in_specs=[
            pl.BlockSpec((seq, d_model), lambda i, j: (i, 0)),
            pl.BlockSpec((1, d_model), lambda i, j: (0, 0)),
            pl.BlockSpec((None, d_model, step), lambda i, j: (layer, 0, jnp.minimum(j, S_FOX))),
            pl.BlockSpec((None, d_model, step), lambda i, j: (layer, 0, jnp.maximum(j - S_SB, 0))),
            pl.BlockSpec((None, d_model, LANES), lambda i, j: (layer, 0, 0)),
            pl.BlockSpec((1, step), lambda i, j: (0, jnp.maximum(j - S_GATE, 0))),
            pl.BlockSpec((1, LANES), lambda i, j: (0, 0)),
            pl.BlockSpec((CONV_K, WIDTH), lambda i, j: (0, 0)),
            pl.BlockSpec((1, 2 * TN_IN), lambda i, j: (0, 0)),
            pl.BlockSpec((CUM_BLK, CUM_BLK), lambda i, j: (0, 0)),
        ],
        out_specs=[
            pl.BlockSpec((seq, step), lambda i, j: (i, jnp.maximum(j - S_FOX, 0))),
            pl.BlockSpec((seq, WIDTH), lambda i, j: (i, 0)),
            pl.BlockSpec((seq, LANES), lambda i, j: (i, 0)),
        ],
        out_shape=[
            jax.ShapeDtypeStruct((tokens, n_out), BF16),
            jax.ShapeDtypeStruct((tokens, WIDTH), BF16),
            jax.ShapeDtypeStruct((tokens, LANES), F32),
        ],
        scratch_shapes=[
            pltpu.VMEM((seq, d_model), BF16),
            pltpu.VMEM((seq + HALO, WIDTH), F32),
        ],
        compiler_params=_cparams(("arbitrary", "arbitrary")),
        name="in_proj",
    )(x2d, g1, w_a, w_b, w_f, gate_bias, f_bias, conv_w, qk_gain, tri)


def _pair_masks(q_pair):
    lane = lax.broadcasted_iota(jnp.int32, q_pair.shape, 1)
    zero = jnp.zeros_like(q_pair)
    return (jnp.where(lane < HEAD_DIM, q_pair, zero), jnp.where(lane < HEAD_DIM, zero, q_pair))


def _merge_pair(o0, o1):
    return jnp.concatenate([o0[0:HEAD_DIM, :], o1[HEAD_DIM:2 * HEAD_DIM, :]], axis=0)


def _store_masked_queries(q_ref, qm_scr):
    for p in range(N_HEADS // 2):
        qm = _pair_masks(q_ref[:, p * LANES:(p + 1) * LANES])
        qm_scr[2 * p] = qm[0]
        qm_scr[2 * p + 1] = qm[1]


def _pair_cols(h):
    return slice((h // 2) * LANES, (h // 2 + 1) * LANES)


def _run_pipelined(qi, q0, scores, vector_stage, value_stage):
    vector_stage(scores(q0), q0, True)

    def pair_body(i, _):
        k_a = pl.multiple_of((qi - 1 - 2 * i) * TK, TK)
        k_b = pl.multiple_of(k_a - TK, TK)
        s_a = scores(k_a)
        value_stage(pl.multiple_of(k_a + TK, TK))
        s_b = scores(k_b)
        vector_stage(s_a, k_a, False)
        value_stage(k_a)
        vector_stage(s_b, k_b, False)
        return 0
    lax.fori_loop(0, qi // 2, pair_body, 0)

    @pl.when(qi % 2 == 1)
    def _():
        s_all = scores(0)
        value_stage(TK)
        vector_stage(s_all, 0, False)
    value_stage(0)


def _fox_kernel(q_ref, k_ref, v_ref, c_ref, o_ref, qm_scr, acc_scr, p_scr, m_scr, l_scr, alpha_scr):
    qi = pl.program_id(1)
    q0 = pl.multiple_of(qi * TQ, TQ)
    krow = lax.broadcasted_iota(jnp.int32, (TK, TQ), 0)
    qcol = lax.broadcasted_iota(jnp.int32, (TK, TQ), 1)
    causal = krow <= qcol
    heads = range(N_HEADS)

    _store_masked_queries(q_ref, qm_scr)
    acc_scr[...] = jnp.zeros(acc_scr.shape, F32)
    m_scr[...] = jnp.full(m_scr.shape, NEG_INF, F32)
    l_scr[...] = jnp.zeros(l_scr.shape, F32)

    def scores(k0):
        return [_dot_nt(k_ref[pl.ds(k0, TK), _pair_cols(h)], qm_scr[h]) for h in heads]

    def vector_stage(s_all, k0, diag):
        for h in heads:
            s = s_all[h] - c_ref[pl.ds(k0, TK), h:h + 1]
            if diag:
                s = jnp.where(causal, s, NEG_INF)
            m = m_scr[h:h + 1, :]
            m_new = jnp.maximum(m, jnp.max(s, axis=0, keepdims=True))
            alpha = jnp.exp2(m - m_new)
            pr = jnp.exp2(s - m_new)
            m_scr[h:h + 1, :] = m_new
            l_scr[h:h + 1, :] = alpha * l_scr[h:h + 1, :] + jnp.sum(pr, axis=0, keepdims=True)
            alpha_scr[h:h + 1, :] = alpha
            p_scr[h] = pr.astype(BF16)

    def value_stage(k0):
        for h in heads:
            pv = _dot_tn(v_ref[pl.ds(k0, TK), _pair_cols(h)], p_scr[h])
            acc_scr[h] = alpha_scr[h:h + 1, :] * acc_scr[h] + pv

    _run_pipelined(qi, q0, scores, vector_stage, value_stage)
    for p in range(N_HEADS // 2):
        o_pair = _merge_pair(acc_scr[2 * p] / l_scr[2 * p:2 * p + 1, :],
                             acc_scr[2 * p + 1] / l_scr[2 * p + 1:2 * p + 2, :])
        o_ref[:, p * LANES:(p + 1) * LANES] = o_pair.T.astype(BF16)


def _sb_kernel(q_ref, k_ref, v_ref, tri_ref, o_ref, qm_scr, acc_scr, p_scr, r_scr):
    qi = pl.program_id(1)
    q0 = pl.multiple_of(qi * TQ, TQ)
    krow = lax.broadcasted_iota(jnp.int32, (TK, TQ), 0)
    qcol = lax.broadcasted_iota(jnp.int32, (TK, TQ), 1)
    strict = krow < qcol
    heads = range(N_HEADS)

    _store_masked_queries(q_ref, qm_scr)
    acc_scr[...] = jnp.zeros(acc_scr.shape, F32)
    r_scr[...] = jnp.zeros(r_scr.shape, F32)

    def scores(k0):
        return [_dot_nt(k_ref[pl.ds(k0, TK), _pair_cols(h)], qm_scr[h]) for h in heads]

    def vector_stage(z_all, k0, diag):
        la_all = []
        for h in heads:
            z = z_all[h]
            neg_abs = lax.bitcast_convert_type(
                lax.bitcast_convert_type(z, jnp.uint32) | jnp.uint32(0x80000000), F32)
            u = jnp.maximum(z, 0.0) + jnp.log(1.0 + jnp.exp2(neg_abs)) * LOG2E
            if diag:
                u = jnp.where(strict, u, 0.0)
            cs = _dot(tri_ref[...], u.astype(BF16))
            r = r_scr[h:h + 1, :]
            r_scr[h:h + 1, :] = r + cs[0:1, :]
            la_all.append((z - r) - cs)
        for h in heads:
            a = jnp.exp2(la_all[h])
            if diag:
                a = jnp.where(strict, a, 0.0)
            p_scr[h] = a.astype(BF16)

    def value_stage(k0):
        for h in heads:
            acc_scr[h] = acc_scr[h] + _dot_tn(v_ref[pl.ds(k0, TK), _pair_cols(h)], p_scr[h])

    _run_pipelined(qi, q0, scores, vector_stage, value_stage)
    for p in range(N_HEADS // 2):
        o_pair = _merge_pair(acc_scr[2 * p], acc_scr[2 * p + 1])
        o_ref[:, p * LANES:(p + 1) * LANES] = o_pair.T.astype(BF16)


def _fox_attention(main, c, batch, seq):
    tokens = main.shape[0]
    nq = seq // TQ
    return pl.pallas_call(
        _fox_kernel,
        grid=(batch, nq),
        in_specs=[
            pl.BlockSpec((TQ, WIDTH), lambda b, qi: (b * nq + qi, 0)),
            pl.BlockSpec((seq, WIDTH), lambda b, qi: (b, 1)),
            pl.BlockSpec((seq, WIDTH), lambda b, qi: (b, 2)),
            pl.BlockSpec((seq, LANES), lambda b, qi: (b, 0)),
        ],
        out_specs=pl.BlockSpec((TQ, WIDTH), lambda b, qi: (b * nq + qi, 0)),
        out_shape=jax.ShapeDtypeStruct((tokens, WIDTH), BF16),
        scratch_shapes=[pltpu.VMEM((N_HEADS, TQ, LANES), BF16), pltpu.VMEM((N_HEADS, LANES, TQ), F32),
                        pltpu.VMEM((N_HEADS, TK, TQ), BF16), pltpu.VMEM((N_HEADS, TQ), F32),
                        pltpu.VMEM((N_HEADS, TQ), F32), pltpu.VMEM((N_HEADS, TQ), F32)],
        compiler_params=_cparams(("arbitrary", "arbitrary")),
        name="fox_attention",
    )(main, main, main, c)


def _sb_attention(main, tri, batch, seq):
    tokens = main.shape[0]
    nq = seq // TQ
    return pl.pallas_call(
        _sb_kernel,
        grid=(batch, nq),
        in_specs=[
            pl.BlockSpec((TQ, WIDTH), lambda b, qi: (b * nq + qi, 3)),
            pl.BlockSpec((seq, WIDTH), lambda b, qi: (b, 4)),
            pl.BlockSpec((seq, WIDTH), lambda b, qi: (b, 5)),
            pl.BlockSpec((TK, TK), lambda b, qi: (0, 0)),
        ],
        out_specs=pl.BlockSpec((TQ, WIDTH), lambda b, qi: (b * nq + qi, 0)),
        out_shape=jax.ShapeDtypeStruct((tokens, WIDTH), BF16),
        scratch_shapes=[pltpu.VMEM((N_HEADS, TQ, LANES), BF16), pltpu.VMEM((N_HEADS, LANES, TQ), F32),
                        pltpu.VMEM((N_HEADS, TK, TQ), BF16), pltpu.VMEM((N_HEADS, TQ), F32)],
        compiler_params=_cparams(("arbitrary", "arbitrary")),
        name="sb_attention",
    )(main, main, main, tri)


def _mix_kernel(yc_ref, af_ref, as_ref, gate_ref, x_ref, wc_ref, wf_ref, ws_ref, wo_ref, g2_ref,
                xo_ref, hn_ref, y_scr):
    d_model = x_ref.shape[1]
    chunks = [slice(r * MM_ROWS, (r + 1) * MM_ROWS) for r in range(x_ref.shape[0] // MM_ROWS)]
    for rows in chunks:
        y = gate_ref[rows, 0:d_model].astype(F32) * _dot(yc_ref[rows, :], wc_ref[...])
        y = y + gate_ref[rows, d_model:2 * d_model].astype(F32) * _dot(af_ref[rows, :], wf_ref[...])
        y = y + gate_ref[rows, 2 * d_model:3 * d_model].astype(F32) * _dot(as_ref[rows, :], ws_ref[...])
        y_scr[rows, :] = y.astype(BF16)
    for rows in chunks:
        xn = x_ref[rows, :] + _dot(y_scr[rows, :], wo_ref[...])
        xo_ref[rows, :] = xn
        ms = jnp.mean(xn * xn, axis=-1, keepdims=True)
        hn_ref[rows, :] = (xn * lax.rsqrt(ms + NORM_EPS) * g2_ref[...]).astype(BF16)


def _mix(layer, yc, af, asb, main, x2d, wc, wf, ws, wo, g2):
    tokens, d_model = x2d.shape
    gate_blk = (main.shape[1] - 3 * d_model) // (3 * d_model)
    row = lambda i: (i, 0)
    const = lambda i: (0, 0)
    stacked = lambda i: (layer, 0, 0)
    return pl.pallas_call(
        _mix_kernel,
        grid=(tokens // TM_MIX,),
        in_specs=[
            pl.BlockSpec((TM_MIX, WIDTH), row),
            pl.BlockSpec((TM_MIX, WIDTH), row),
            pl.BlockSpec((TM_MIX, WIDTH), row),
            pl.BlockSpec((TM_MIX, 3 * d_model), lambda i: (i, gate_blk)),
            pl.BlockSpec((TM_MIX, d_model), row),
            pl.BlockSpec((None, WIDTH, d_model), stacked),
            pl.BlockSpec((None, WIDTH, d_model), stacked),
            pl.BlockSpec((None, WIDTH, d_model), stacked),
            pl.BlockSpec((None, d_model, d_model), stacked),
            pl.BlockSpec((1, d_model), const),
        ],
        out_specs=[pl.BlockSpec((TM_MIX, d_model), row), pl.BlockSpec((TM_MIX, d_model), row)],
        out_shape=[jax.ShapeDtypeStruct((tokens, d_model), F32),
                   jax.ShapeDtypeStruct((tokens, d_model), BF16)],
        scratch_shapes=[pltpu.VMEM((TM_MIX, d_model), BF16)],
        compiler_params=_cparams(("arbitrary",)),
        name="mix",
    )(yc, af, asb, main, x2d, wc, wf, ws, wo, g2)


def _ffn_kernel(hn_ref, wg_ref, wv_ref, cw_ref, cb_ref, wd_ref, x_ref, o_ref, ug_scr, uv_scr):
    j = pl.program_id(1)
    seq = hn_ref.shape[0]
    ug_scr[0:HALO, :] = jnp.zeros((HALO, ug_scr.shape[1]), F32)

    def run(first):
        n_chunks = seq // MM_ROWS

        def up(r):
            rows = slice(r * MM_ROWS, (r + 1) * MM_ROWS)
            base = r * MM_ROWS + HALO
            ug_scr[base:base + MM_ROWS, :] = _dot(hn_ref[rows, :], wg_ref[...])
            uv_scr[rows, :] = _dot(hn_ref[rows, :], wv_ref[...])

        def down(r):
            rows = slice(r * MM_ROWS, (r + 1) * MM_ROWS)
            base = r * MM_ROWS + HALO
            u = cw_ref[2:3, :] * ug_scr[base:base + MM_ROWS, :] + cb_ref[...]
            u = u + cw_ref[1:2, :] * ug_scr[base - 1:base - 1 + MM_ROWS, :]
            u = u + cw_ref[0:1, :] * ug_scr[base - 2:base - 2 + MM_ROWS, :]
            part = _dot((u * _sigmoid(u) * uv_scr[rows, :]).astype(BF16), wd_ref[...])
            o_ref[rows, :] = (x_ref[rows, :] if first else o_ref[rows, :]) + part

        up(0)
        for r in range(n_chunks):
            if r + 1 < n_chunks:
                up(r + 1)
            down(r)

    @pl.when(j == 0)
    def _():
        run(True)

    @pl.when(j > 0)
    def _():
        run(False)


def _ffn(layer, hn2, w_up, conv_w, conv_b, w_down, x2d, batch, seq):
    tokens, d_model = hn2.shape
    d_ff = w_up.shape[2] // 2
    nj = d_ff // TN_UP
    return pl.pallas_call(
        _ffn_kernel,
        grid=(batch, nj),
        in_specs=[
            pl.BlockSpec((seq, d_model), lambda b, j: (b, 0)),
            pl.BlockSpec((None, d_model, TN_UP), lambda b, j: (layer, 0, j)),
            pl.BlockSpec((None, d_model, TN_UP), lambda b, j: (layer, 0, nj + j)),
            pl.BlockSpec((CONV_K, TN_UP), lambda b, j: (0, j)),
            pl.BlockSpec((1, TN_UP), lambda b, j: (0, j)),
            pl.BlockSpec((None, TN_UP, d_model), lambda b, j: (layer, j, 0)),
            pl.BlockSpec((seq, d_model), lambda b, j: (b, 0)),
        ],
        out_specs=pl.BlockSpec((seq, d_model), lambda b, j: (b, 0)),
        out_shape=jax.ShapeDtypeStruct((tokens, d_model), F32),
        scratch_shapes=[pltpu.VMEM((seq + HALO, TN_UP), F32), pltpu.VMEM((seq, TN_UP), F32)],
        compiler_params=_cparams(("arbitrary", "arbitrary")),
        name="ffn",
    )(hn2, w_up, w_up, conv_w, conv_b, w_down, x2d)


def kernel(x, norm1_g, w_in, fox_f_bias, gate_bias, conv_w, fox_q_norm_g, fox_k_norm_g, w_proj_conv,
           w_proj_fox, w_proj_sb, w_out, norm2_g, w_up, ffn_conv_w, ffn_conv_b, w_down):
    batch, seq, d_model = x.shape
    depth = w_in.shape[0]
    scale = HEAD_DIM ** -0.5 * LOG2E
    assert seq % max(TQ, TK, CUM_BLK, ROW_CHUNK) == 0 and TQ == TK

    o_f = 6 * WIDTH
    o_sb = o_f + N_HEADS
    w_a = w_in.astype(BF16)
    w_b = w_a[:, :, o_sb:]
    w_f = jnp.pad(w_a[:, :, o_f:o_sb], ((0, 0), (0, 0), (0, LANES - N_HEADS)))
    f_bias = jnp.pad(fox_f_bias, ((0, 0), (0, LANES - N_HEADS)))[:, None, :]
    qk_gain = jnp.concatenate([jnp.tile(fox_q_norm_g * scale, (1, N_HEADS)),
                               jnp.tile(fox_k_norm_g, (1, N_HEADS))], axis=-1)[:, None, :]
    wc, wf, ws, wo = (w.astype(BF16) for w in (w_proj_conv, w_proj_fox, w_proj_sb, w_out))
    w_up_b, w_down_b = w_up.astype(BF16), w_down.astype(BF16)

    idx = jnp.arange(CUM_BLK)
    tri_incl = (idx[None, :] <= idx[:, None]).astype(BF16)
    tri_suffix = (idx[None, :] >= idx[:, None]).astype(BF16)

    x2d = x.reshape(batch * seq, d_model)
    for l in range(depth):
        main, yc, c = _in_proj(l, x2d, norm1_g[l][None, :], w_a, w_b, w_f, gate_bias[l][None, :], f_bias[l],
                               conv_w[l], qk_gain[l], tri_incl, batch, seq, scale)
        af = _fox_attention(main, c, batch, seq)
        asb = _sb_attention(main, tri_suffix, batch, seq)
        x2d, hn2 = _mix(l, yc, af, asb, main, x2d, wc, wf, ws, wo, norm2_g[l][None, :])
        x2d = _ffn(l, hn2, w_up_b, ffn_conv_w[l], ffn_conv_b[l][None, :], w_down_b, x2d, batch, seq)
    return x2d.reshape(batch, seq, d_model)
```
